```python
import jax, jax.numpy as jnp
from jax import lax
import numpy as np

D_MODEL = 1024
BATCH = 16
SEQ = 2048
DEPTH = 4
DEC_BATCH = 8
DEC_SEQ = 64
PAST_LEN = 1024

CHUNK = 64
N_EVEN = (DEPTH + 1) // 2
N_ODD = DEPTH // 2
EPS = 1e-6
ATTN_DIM = D_MODEL // 2
N_HEADS_A = 8
HEAD_DIM = ATTN_DIM // N_HEADS_A
N_KV_A = 2
GQA_REP = N_HEADS_A // N_KV_A
KV_DIM = N_KV_A * HEAD_DIM
N_IDX_HEADS = 8
IDX_DIM = 64
TOPK_MAX = 256
Q_BLOCK = 128
CONV_DIM = D_MODEL // 2
CONV_W = 3
SGU_DIM = D_MODEL
SGU_CHUNK = 128
N_SGU_GROUPS = 8
SGU_GROUP_CH = SGU_DIM // N_SGU_GROUPS

EVEN_SPLITS = (ATTN_DIM, KV_DIM, KV_DIM, ATTN_DIM, N_IDX_HEADS * IDX_DIM, IDX_DIM, N_IDX_HEADS,
               CONV_DIM, CONV_DIM, CONV_DIM, CONV_DIM)
EVEN_IN = sum(EVEN_SPLITS)
ODD_IN = 3 * SGU_DIM

kernel_name = "dsa_shortconv_sgu_streaming_step"


def _split(t, sizes):
    cuts = [int(i) for i in np.cumsum(sizes)[:-1]]
    return jnp.split(t, cuts, axis=-1)


def _rmsnorm(x, g):
    xf = x.astype(jnp.float32)
    y = xf * lax.rsqrt(jnp.mean(xf * xf, axis=-1, keepdims=True) + EPS)
    return (y * g.astype(jnp.float32)).astype(x.dtype)


def _layernorm(x, g, b):
    xf = x.astype(jnp.float32)
    mu = jnp.mean(xf, axis=-1, keepdims=True)
    xc = xf - mu
    y = xc * lax.rsqrt(jnp.mean(xc * xc, axis=-1, keepdims=True) + EPS)
    return (y * g.astype(jnp.float32) + b.astype(jnp.float32)).astype(x.dtype)


def _dsa_block(q, qi, wi, qpos, k, v, ki, kpos, topk):
    rel = jax.nn.relu(jnp.einsum('bqhd,bsd->bqhs', qi, ki)).astype(jnp.float32)
    score = jnp.einsum('bqh,bqhs->bqs', wi.astype(jnp.float32), rel)
    adm = (kpos[None, :] // CHUNK) <= (qpos[:, None] // CHUNK)
    score = jnp.where(adm[None], score, -jnp.inf)
    top_val, top_idx = lax.top_k(score, topk)
    valid = jnp.isfinite(top_val)
    gather = jax.vmap(lambda a, i: a[i])
    kg = gather(k, top_idx)
    vg = gather(v, top_idx)
    logits = jnp.einsum('bqgrd,bqkgd->bqgrk', q, kg).astype(jnp.float32) * (HEAD_DIM ** -0.5)
    logits = jnp.where(valid[:, :, None, None, :], logits, -jnp.inf)
    p = jax.nn.softmax(logits, axis=-1).astype(v.dtype)
    return jnp.einsum('bqgrk,bqkgd->bqgrd', p, vg)


def _even_mixer(h, w_in, conv_w, w_out, past_k, past_v, past_ki, conv_hist):
    nb, T, _ = h.shape
    q, k, v, ga, qi, ki, wi, gb, gc, xin, gz = _split(h @ w_in, EVEN_SPLITS)
    q = q.reshape(nb, T, N_KV_A, GQA_REP, HEAD_DIM)
    k = k.reshape(nb, T, N_KV_A, HEAD_DIM)
    v = v.reshape(nb, T, N_KV_A, HEAD_DIM)
    qi = qi.reshape(nb, T, N_IDX_HEADS, IDX_DIM)
    if past_k is None:
        P = 0
        keys, vals, kidx = k, v, ki
        hist = jnp.zeros((nb, CONV_W - 1, CONV_DIM), h.dtype)
    else:
        P = past_k.shape[1]
        keys = jnp.concatenate([past_k.astype(k.dtype), k], axis=1)
        vals = jnp.concatenate([past_v.astype(v.dtype), v], axis=1)
        kidx = jnp.concatenate([past_ki.astype(ki.dtype), ki], axis=1)
        hist = conv_hist.astype(h.dtype)
    L = P + T
    kpos = jnp.arange(L, dtype=jnp.int32)
    qpos = P + jnp.arange(T, dtype=jnp.int32)
    topk = min(TOPK_MAX, L // 4)
    if T > Q_BLOCK:
        nblk = T // Q_BLOCK
        def blk(a):
            return jnp.moveaxis(a.reshape((nb, nblk, Q_BLOCK) + a.shape[2:]), 1, 0)
        o = lax.map(lambda xs: _dsa_block(xs[0], xs[1], xs[2], xs[3], keys, vals, kidx, kpos, topk),
                    (blk(q), blk(qi), blk(wi), qpos.reshape(nblk, Q_BLOCK)))
        o = jnp.moveaxis(o, 0, 1).reshape(nb, T, ATTN_DIM)
    else:
        o = _dsa_block(q, qi, wi, qpos, keys, vals, kidx, kpos, topk).reshape(nb, T, ATTN_DIM)
    a_out = o * jax.nn.silu(ga)
    z = gc * xin
    zp = jnp.concatenate([hist, z], axis=1)
    y = conv_w[CONV_W - 1] * zp[:, CONV_W - 1:]
    for j in range(CONV_W - 1):
        y = y + conv_w[j] * zp[:, j:j + T]
    b_out = gb * y * jax.nn.silu(gz)
    out = jnp.concatenate([a_out, b_out], axis=-1) @ w_out
    return out, k, v, ki, zp[:, -(CONV_W - 1):]


def _odd_mixer(h, w_in, ws, bs, ln_g, ln_b, w_out):
    nb, T, _ = h.shape
    u, v, g = _split(h @ w_in, (SGU_DIM, SGU_DIM, SGU_DIM))
    u = jax.nn.gelu(u)
    v = _layernorm(jax.nn.gelu(v), ln_g, ln_b)
    if T >= SGU_CHUNK:
        nc, lc = T // SGU_CHUNK, SGU_CHUNK
    else:
        nc, lc = 1, T
    vc = v.reshape(nb, nc, lc, N_SGU_GROUPS, SGU_GROUP_CH)
    wm = jnp.tril(ws[:, :lc, :lc])
    s = jnp.einsum('gij,bnjgc->bnigc', wm, vc) + bs[:, :lc].T[None, None, :, :, None]
    s = s.reshape(nb, T, SGU_DIM)
    out = (u * s * jax.nn.silu(g)) @ w_out
    return out, v


def _trunk(x, c, ada_w, ada_b, norm_g, ev_w_in, ev_conv_w, ev_w_out, od_w_in, od_ws, od_bs, od_ln_g, od_ln_b,
           od_w_out, final_g, cache_k=None, cache_v=None, cache_ki=None, conv_state=None):
    ks, vs, kis, convs, cvs = [], [], [], [], []
    cs = jax.nn.silu(c)
    for l in range(DEPTH):
        shift, scale, gate = jnp.split(cs @ ada_w[l] + ada_b[l], 3, axis=-1)
        h = _rmsnorm(x, norm_g[l]) * (1 + scale[:, None]) + shift[:, None]
        if l % 2 == 0:
            e = l // 2
            if cache_k is None:
                hist = (None, None, None, None)
            else:
                hist = (cache_k[e], cache_v[e], cache_ki[e], conv_state[e])
            out, k, v, ki, cst = _even_mixer(h, ev_w_in[e], ev_conv_w[e], ev_w_out[e], *hist)
            ks.append(k); vs.append(v); kis.append(ki); convs.append(cst)
        else:
            o = l // 2
            out, vrow = _odd_mixer(h, od_w_in[o], od_ws[o], od_bs[o], od_ln_g[o], od_ln_b[o], od_w_out[o])
            cvs.append(vrow)
        x = x + gate[:, None] * out
    new_c = None if cache_k is None else jnp.stack(cvs)
    return _rmsnorm(x, final_g), jnp.stack(ks), jnp.stack(vs), jnp.stack(kis), jnp.stack(convs), new_c


def setup_inputs(seed: int = 0) -> dict:
    key = jax.random.key(seed)
    k = jax.random.split(key, 21)
    def nrm(kk, shape, s):
        return jax.random.normal(kk, shape, jnp.float32) * s
    return {
        "x_prompt": nrm(k[0], (BATCH, SEQ, D_MODEL), 1.0),
        "x_sample": nrm(k[1], (DEC_BATCH, DEC_SEQ, D_MODEL), 1.0),
        "cache_a_k": nrm(k[2], (N_EVEN, DEC_BATCH, PAST_LEN, N_KV_A, HEAD_DIM), 1.0),
        "cache_a_v": nrm(k[3], (N_EVEN, DEC_BATCH, PAST_LEN, N_KV_A, HEAD_DIM), 1.0),
        "cache_a_kidx": nrm(k[4], (N_EVEN, DEC_BATCH, PAST_LEN, IDX_DIM), 1.0),
        "state_b_conv": nrm(k[5], (N_EVEN, DEC_BATCH, CONV_W - 1, CONV_DIM), 1.0),
        "c_prompt": nrm(k[6], (BATCH, D_MODEL), 1.0),
        "c_sample": nrm(k[7], (DEC_BATCH, D_MODEL), 1.0),
        "ada_w": nrm(k[8], (DEPTH, D_MODEL, 3 * D_MODEL), 0.5 * D_MODEL ** -0.5),
        "ada_b": nrm(k[9], (DEPTH, 3 * D_MODEL), 0.02),
        "norm_g": 1.0 + nrm(k[10], (DEPTH, D_MODEL), 0.02),
        "ev_w_in": nrm(k[11], (N_EVEN, D_MODEL, EVEN_IN), D_MODEL ** -0.5),
        "ev_conv_w": nrm(k[12], (N_EVEN, CONV_W, CONV_DIM), CONV_W ** -0.5),
        "ev_w_out": nrm(k[13], (N_EVEN, ATTN_DIM + CONV_DIM, D_MODEL), (ATTN_DIM + CONV_DIM) ** -0.5),
        "od_w_in": nrm(k[14], (N_ODD, D_MODEL, ODD_IN), D_MODEL ** -0.5),
        "od_ws": nrm(k[15], (N_ODD, N_SGU_GROUPS, SGU_CHUNK, SGU_CHUNK), 0.5 * SGU_CHUNK ** -0.5),
        "od_bs": 1.0 + nrm(k[16], (N_ODD, N_SGU_GROUPS, SGU_CHUNK), 0.1),
        "od_ln_g": 1.0 + nrm(k[17], (N_ODD, SGU_DIM), 0.02),
        "od_ln_b": nrm(k[18], (N_ODD, SGU_DIM), 0.02),
        "od_w_out": nrm(k[19], (N_ODD, SGU_DIM, D_MODEL), SGU_DIM ** -0.5),
        "final_g": 1.0 + nrm(k[20], (D_MODEL,), 0.02),
    }


def reference(x_prompt, x_sample, cache_a_k, cache_a_v, cache_a_kidx, state_b_conv, c_prompt, c_sample,
              ada_w, ada_b, norm_g, ev_w_in, ev_conv_w, ev_w_out, od_w_in, od_ws, od_bs, od_ln_g, od_ln_b,
              od_w_out, final_g):
    y_prompt, k_p, v_p, ki_p, conv_p, _ = _trunk(
        x_prompt, c_prompt, ada_w, ada_b, norm_g, ev_w_in, ev_conv_w, ev_w_out, od_w_in, od_ws, od_bs,
        od_ln_g, od_ln_b, od_w_out, final_g)
    y_sample, k_s, v_s, ki_s, conv_s, cv_s = _trunk(
        x_sample, c_sample, ada_w, ada_b, norm_g, ev_w_in, ev_conv_w, ev_w_out, od_w_in, od_ws, od_bs,
        od_ln_g, od_ln_b, od_w_out, final_g, cache_a_k, cache_a_v, cache_a_kidx, state_b_conv)
    return (y_prompt, y_sample, k_p, v_p, ki_p, conv_p, k_s, v_s, ki_s, conv_s, cv_s)
```

```python
import functools

import jax
import jax.numpy as jnp
from jax import lax
from jax.experimental import pallas as pl
from jax.experimental.pallas import tpu as pltpu

F32 = jnp.float32
BF16 = jnp.bfloat16

D_MODEL = 1024
EPS = 1e-6
CHUNK_SHIFT = 6
CHUNK = 1 << CHUNK_SHIFT
ATTN_DIM = 512
N_HEADS = 8
HEAD_DIM = 64
N_KV = 2
GQA_REP = N_HEADS // N_KV
N_IDX_HEADS = 8
IDX_DIM = 64
TOPK_MAX = 256
CONV_DIM = 512
CONV_W = 3
SGU_DIM = 1024
SGU_CHUNK = 128
N_SGU_GROUPS = 8
SGU_GROUP_CH = SGU_DIM // N_SGU_GROUPS

SUBLANES = 8
LANES = 128
VMEM_LIMIT_BYTES = 52 * 1024 * 1024

KEY_TILE = 128
BISECT_STEPS = 32
NEG_SCORE = -3.0e38
POS_SCORE = 3.0e38
NEG_BIAS = -1.0e30

NT_Q, NT_GA, NT_QI, NT_V, NT_WI, NT_ROWS = 0, 512, 1024, 1536, 1664, 1680
NN_GB, NN_GC, NN_XIN, NN_GZ, NN_K, NN_V, NN_KI, NN_COLS = 0, 512, 1024, 1536, 2048, 2176, 2304, 2432


def _silu(x):
    return x / (1.0 + jnp.exp(-x))


def _fold_rows(a, op):
    r, c = a.shape
    return op(a.reshape(r // SUBLANES, SUBLANES, c), axis=0)


def _modulated_norm(x, g, sc1p, sh):
    ms = jnp.mean(x * x, axis=-1, keepdims=True)
    return (x * lax.rsqrt(ms + EPS) * g) * sc1p + sh


def _ada_kernel(c_ref, w_ref, b_ref, o_ref):
    cs = _silu(c_ref[...])
    o_ref[...] = jnp.dot(cs, w_ref[...], preferred_element_type=F32,
                         precision=lax.Precision.HIGHEST) + b_ref[...]


def _ada_call(c_all, ada_w, ada_b):
    depth, d, n3 = ada_w.shape
    nb = c_all.shape[0]
    bn = 1024
    return pl.pallas_call(
        _ada_kernel,
        grid=(depth, n3 // bn),
        in_specs=[
            pl.BlockSpec((nb, d), lambda l, n: (0, 0)),
            pl.BlockSpec((None, d, bn), lambda l, n: (l, 0, n)),
            pl.BlockSpec((None, 1, bn), lambda l, n: (l, 0, n)),
        ],
        out_specs=pl.BlockSpec((None, nb, bn), lambda l, n: (l, 0, n)),
        out_shape=jax.ShapeDtypeStruct((depth, nb, n3), F32),
        compiler_params=pltpu.CompilerParams(
            dimension_semantics=("arbitrary", "arbitrary"), vmem_limit_bytes=VMEM_LIMIT_BYTES),
        name="ada_mod",
    )(c_all, ada_w, ada_b.reshape(depth, 1, n3))


def _even_in_kernel(x_ref, sc_ref, sh_ref, g_ref, wnn_ref, wnt_ref, cw_ref, hist_ref,
                    qT_ref, gaT_ref, qiT_ref, vT_ref, wiT_ref, k_ref, v_ref, ki_ref,
                    kb_ref, kib_ref, bo_ref, cst_ref, zbuf, *, tm, vc):
    t = pl.program_id(1)
    hb = _modulated_norm(x_ref[...], g_ref[...], sc_ref[...], sh_ref[...]).astype(BF16)

    def nn(c0, c1):
        return jnp.dot(hb, wnn_ref[:, c0:c1], preferred_element_type=F32)

    def nt(r0, r1):
        return lax.dot_general(wnt_ref[r0:r1, :], hb, (((1,), (1,)), ((), ())),
                               preferred_element_type=F32)

    z = nn(NN_GC, NN_XIN) * nn(NN_XIN, NN_GZ)

    @pl.when(t == 0)
    def _():
        zbuf[0:SUBLANES, :] = jnp.zeros((SUBLANES, CONV_DIM), F32)
        zbuf[SUBLANES - 2:SUBLANES, :] = hist_ref[...]

    zbuf[SUBLANES:SUBLANES + tm, :] = z
    z1 = zbuf[SUBLANES - 1:SUBLANES - 1 + tm, :]
    z2 = zbuf[SUBLANES - 2:SUBLANES - 2 + tm, :]
    y = cw_ref[2:3, :] * z + cw_ref[1:2, :] * z1 + cw_ref[0:1, :] * z2
    last2 = zbuf[SUBLANES + tm - 2:SUBLANES + tm, :]
    zbuf[SUBLANES - 2:SUBLANES, :] = last2
    cst_ref[...] = last2
    bo_ref[...] = (nn(NN_GB, NN_GC) * y * _silu(nn(NN_GZ, NN_K))).astype(BF16)

    kvk = nn(NN_K, NN_COLS)
    k = kvk[:, 0:128]
    k_ref[...] = k
    v_ref[...] = kvk[:, 128:256]
    ki = kvk[:, 256:256 + IDX_DIM]
    ki_ref[...] = ki
    kib_ref[...] = ki.astype(BF16)
    for g in range(N_KV):
        kb_ref[g] = k[:, g * HEAD_DIM:(g + 1) * HEAD_DIM].astype(BF16)

    qT_ref[...] = (nt(NT_Q, NT_GA) * (HEAD_DIM ** -0.5)).astype(BF16)
    gaT_ref[...] = _silu(nt(NT_GA, NT_QI))
    qiT_ref[...] = nt(NT_QI, NT_V).astype(BF16)
    vT = nt(NT_V, NT_WI).astype(BF16)
    for c in range(tm // vc):
        vT_ref[c] = vT[:, c * vc:(c + 1) * vc]
    wiT_ref[...] = nt(NT_WI, NT_ROWS)[0:N_IDX_HEADS, :]


def _even_in_call(x, sc1p, sh, g, wnn, wnt, cw, hist, *, tm):
    nb, t_len, d = x.shape
    vc = min(KEY_TILE, tm)
    nt_tiles = t_len // tm
    kern = functools.partial(_even_in_kernel, tm=tm, vc=vc)
    tok = lambda w: pl.BlockSpec((None, tm, w), lambda b, t: (b, t, 0))
    tokT = lambda r: pl.BlockSpec((None, r, tm), lambda b, t: (b, 0, t))
    per_b = lambda r, w: pl.BlockSpec((None, r, w), lambda b, t: (b, 0, 0))
    full2 = lambda a: pl.BlockSpec(a.shape, lambda b, t: (0, 0))
    out_shape = (
        jax.ShapeDtypeStruct((nb, ATTN_DIM, t_len), BF16),
        jax.ShapeDtypeStruct((nb, ATTN_DIM, t_len), F32),
        jax.ShapeDtypeStruct((nb, N_IDX_HEADS * IDX_DIM, t_len), BF16),
        jax.ShapeDtypeStruct((nb, t_len // vc, N_KV * HEAD_DIM, vc), BF16),
        jax.ShapeDtypeStruct((nb, N_IDX_HEADS, t_len), F32),
        jax.ShapeDtypeStruct((nb, t_len, N_KV * HEAD_DIM), F32),
        jax.ShapeDtypeStruct((nb, t_len, N_KV * HEAD_DIM), F32),
        jax.ShapeDtypeStruct((nb, t_len, IDX_DIM), F32),
        jax.ShapeDtypeStruct((nb, N_KV, t_len, HEAD_DIM), BF16),
        jax.ShapeDtypeStruct((nb, t_len, IDX_DIM), BF16),
        jax.ShapeDtypeStruct((nb, t_len, CONV_DIM), BF16),
        jax.ShapeDtypeStruct((nb, CONV_W - 1, CONV_DIM), F32),
    )
    out_specs = (
        tokT(ATTN_DIM), tokT(ATTN_DIM), tokT(N_IDX_HEADS * IDX_DIM),
        pl.BlockSpec((None, tm // vc, N_KV * HEAD_DIM, vc), lambda b, t: (b, t, 0, 0)),
        tokT(N_IDX_HEADS),
        tok(N_KV * HEAD_DIM), tok(N_KV * HEAD_DIM), tok(IDX_DIM),
        pl.BlockSpec((None, N_KV, tm, HEAD_DIM), lambda b, t: (b, 0, t, 0)),
        tok(IDX_DIM), tok(CONV_DIM), per_b(CONV_W - 1, CONV_DIM),
    )
    return pl.pallas_call(
        kern,
        grid=(nb, nt_tiles),
        in_specs=[tok(d), per_b(1, d), per_b(1, d), full2(g), full2(wnn), full2(wnt), full2(cw),
                  per_b(CONV_W - 1, CONV_DIM)],
        out_specs=out_specs,
        out_shape=out_shape,
        scratch_shapes=[pltpu.VMEM((tm + SUBLANES, CONV_DIM), F32)],
        compiler_params=pltpu.CompilerParams(
            dimension_semantics=("arbitrary", "arbitrary"), vmem_limit_bytes=VMEM_LIMIT_BYTES),
        name="even_in",
    )(x, sc1p, sh, g, wnn, wnt, cw, hist)


def _attn_kernel(qT_ref, gaT_ref, qiT_ref, wiT_ref, kb_ref, kib_ref, vT_ref, x_ref, bo_ref, gate_ref,
                 wo_ref, y_ref, s_ref, lg_ref, *, tq, kt_sz, past, n_keys, n_keys_pad, topk):
    j = pl.program_id(1)
    q0 = past + j * tq
    limit = jnp.minimum((((q0 + tq - 1) >> CHUNK_SHIFT) + 1) << CHUNK_SHIFT, n_keys)
    n_kt = (limit + kt_sz - 1) // kt_sz
    qpos = q0 + lax.broadcasted_iota(jnp.int32, (1, tq), 1)
    qchunk = qpos >> CHUNK_SHIFT
    row_iota = lax.broadcasted_iota(jnp.int32, (kt_sz, tq), 0)
    kf = float(topk)

    def tile_off(kt):
        return pl.multiple_of(kt * kt_sz, kt_sz)

    def score_body(kt, carry):
        mn, mx = carry
        off = tile_off(kt)
        kib = kib_ref[pl.ds(off, kt_sz), :]
        s = jnp.zeros((kt_sz, tq), F32)
        for h in range(N_IDX_HEADS):
            rel = jnp.dot(kib, qiT_ref[h * IDX_DIM:(h + 1) * IDX_DIM, :], preferred_element_type=F32)
            s = s + wiT_ref[h:h + 1, :] * jnp.maximum(rel, 0.0)
        spos = row_iota + off
        adm = ((spos >> CHUNK_SHIFT) <= qchunk) & (spos < n_keys)
        s_lo = jnp.where(adm, s, NEG_SCORE)
        s_ref[pl.ds(off, kt_sz), :] = s_lo
        mn = jnp.minimum(mn, _fold_rows(jnp.where(adm, s, POS_SCORE), jnp.min))
        mx = jnp.maximum(mx, _fold_rows(s_lo, jnp.max))
        return mn, mx

    mn, mx = lax.fori_loop(
        0, n_kt, score_body,
        (jnp.full((SUBLANES, tq), POS_SCORE, F32), jnp.full((SUBLANES, tq), NEG_SCORE, F32)))
    lo0 = jnp.min(mn, axis=0, keepdims=True)
    mx0 = jnp.max(mx, axis=0, keepdims=True)
    hi0 = mx0 + jnp.maximum(jnp.abs(mx0), 1e-30) * (2.0 ** -10)
    n_adm = jnp.minimum((qchunk + 1) << CHUNK_SHIFT, n_keys).astype(F32)

    def count_ge(thr):
        def body(kt, acc):
            s = s_ref[pl.ds(tile_off(kt), kt_sz), :]
            return acc + _fold_rows(jnp.where(s >= thr, 1.0, 0.0), jnp.sum)
        acc = lax.fori_loop(0, n_kt, body, jnp.zeros((SUBLANES, tq), F32))
        return jnp.sum(acc, axis=0, keepdims=True)

    def bisect_body(_, carry):
        lo, hi, c_lo, c_hi = carry
        mid = 0.5 * (lo + hi)
        c = count_ge(mid)
        ge = c >= kf
        return (jnp.where(ge, mid, lo), jnp.where(ge, hi, mid),
                jnp.where(ge, c, c_lo), jnp.where(ge, c_hi, c))

    lo, hi, c_lo, c_hi = lax.fori_loop(
        0, BISECT_STEPS, bisect_body, (lo0, hi0, n_adm, jnp.zeros((1, tq), F32)))

    need = kf - c_hi
    n_j_steps = max(1, (n_keys_pad).bit_length())

    def tie_cut():
        def count_band_le(jcut):
            def body(kt, acc):
                off = tile_off(kt)
                s = s_ref[pl.ds(off, kt_sz), :]
                inb = (s >= lo) & (s < hi) & ((row_iota + off) <= jcut)
                return acc + _fold_rows(jnp.where(inb, 1.0, 0.0), jnp.sum)
            acc = lax.fori_loop(0, n_kt, body, jnp.zeros((SUBLANES, tq), F32))
            return jnp.sum(acc, axis=0, keepdims=True)

        def jbody(_, carry):
            jlo, jhi = carry
            jmid = (jlo + jhi) >> 1
            ok = count_band_le(jmid) >= need
            return jnp.where(ok, jlo, jmid), jnp.where(ok, jmid, jhi)

        _, jhi = lax.fori_loop(
            0, n_j_steps, jbody,
            (jnp.full((1, tq), -1, jnp.int32), jnp.full((1, tq), n_keys_pad - 1, jnp.int32)))
        return jhi

    has_tie = jnp.max(c_lo) > kf
    jcut = lax.cond(has_tie, tie_cut, lambda: jnp.full((1, tq), n_keys_pad - 1, jnp.int32))

    def bias_body(kt, _):
        off = tile_off(kt)
        s = s_ref[pl.ds(off, kt_sz), :]
        sel = (s >= lo) & ((s >= hi) | ((row_iota + off) <= jcut))
        s_ref[pl.ds(off, kt_sz), :] = jnp.where(sel, 0.0, NEG_BIAS)
        return 0

    lax.fori_loop(0, n_kt, bias_body, 0)

    o_rows = []
    for g in range(N_KV):
        qTg = jnp.concatenate(
            [qT_ref[(g * GQA_REP + r) * HEAD_DIM:(g * GQA_REP + r + 1) * HEAD_DIM, :] for r in range(GQA_REP)],
            axis=1)

        def logit_body(kt, m8, g=g, qTg=qTg):
            off = tile_off(kt)
            lt = jnp.dot(kb_ref[g, pl.ds(off, kt_sz), :], qTg, preferred_element_type=F32)
            b = s_ref[pl.ds(off, kt_sz), :]
            lt = lt + jnp.concatenate([b] * GQA_REP, axis=1)
            lg_ref[pl.ds(off, kt_sz), :] = lt
            return jnp.maximum(m8, _fold_rows(lt, jnp.max))

        m8 = lax.fori_loop(0, n_kt, logit_body, jnp.full((SUBLANES, GQA_REP * tq), NEG_SCORE, F32))
        m = jnp.max(m8, axis=0, keepdims=True)

        def pv_body(kt, carry, g=g, m=m):
            acc, l8 = carry
            off = tile_off(kt)
            p = jnp.exp(lg_ref[pl.ds(off, kt_sz), :] - m)
            l8 = l8 + _fold_rows(p, jnp.sum)
            acc = acc + jnp.dot(vT_ref[kt, g * HEAD_DIM:(g + 1) * HEAD_DIM, :], p.astype(BF16),
                                preferred_element_type=F32)
            return acc, l8

        acc, l8 = lax.fori_loop(
            0, n_kt, pv_body,
            (jnp.zeros((HEAD_DIM, GQA_REP * tq), F32), jnp.zeros((SUBLANES, GQA_REP * tq), F32)))
        o = acc / jnp.sum(l8, axis=0, keepdims=True)
        for r in range(GQA_REP):
            o_rows.append(o[:, r * tq:(r + 1) * tq])
    oT = jnp.concatenate(o_rows, axis=0)

    a = (oT * gaT_ref[...]).T.astype(BF16)
    out = jnp.dot(a, wo_ref[0:ATTN_DIM, :], preferred_element_type=F32)
    out = out + jnp.dot(bo_ref[...], wo_ref[ATTN_DIM:ATTN_DIM + CONV_DIM, :], preferred_element_type=F32)
    y_ref[...] = x_ref[...] + gate_ref[...] * out


def _attn_call(qT, gaT, qiT, wiT, kb, kib, vT, x, bo, gate, wo, *, tq, past, n_keys):
    nb, t_len, d = x.shape
    n_keys_pad = kb.shape[2]
    kt_sz = KEY_TILE
    topk = min(TOPK_MAX, n_keys // 4)
    kern = functools.partial(_attn_kernel, tq=tq, kt_sz=kt_sz, past=past, n_keys=n_keys,
                             n_keys_pad=n_keys_pad, topk=topk)
    tokT = lambda r: pl.BlockSpec((None, r, tq), lambda b, j: (b, 0, j))
    tok = lambda w: pl.BlockSpec((None, tq, w), lambda b, j: (b, j, 0))
    return pl.pallas_call(
        kern,
        grid=(nb, t_len // tq),
        in_specs=[
            tokT(ATTN_DIM), tokT(ATTN_DIM), tokT(N_IDX_HEADS * IDX_DIM), tokT(N_IDX_HEADS),
            pl.BlockSpec((None, N_KV, n_keys_pad, HEAD_DIM), lambda b, j: (b, 0, 0, 0)),
            pl.BlockSpec((None, n_keys_pad, IDX_DIM), lambda b, j: (b, 0, 0)),
            pl.BlockSpec((None, n_keys_pad // kt_sz, N_KV * HEAD_DIM, kt_sz), lambda b, j: (b, 0, 0, 0)),
            tok(d), tok(CONV_DIM),
            pl.BlockSpec((None, 1, d), lambda b, j: (b, 0, 0)),
            pl.BlockSpec(wo.shape, lambda b, j: (0, 0)),
        ],
        out_specs=tok(d),
        out_shape=jax.ShapeDtypeStruct((nb, t_len, d), F32),
        scratch_shapes=[pltpu.VMEM((n_keys_pad, tq), F32),
                        pltpu.VMEM((n_keys_pad, GQA_REP * tq), F32)],
        compiler_params=pltpu.CompilerParams(
            dimension_semantics=("arbitrary", "arbitrary"), vmem_limit_bytes=VMEM_LIMIT_BYTES),
        name="dsa_attn",
    )(qT, gaT, qiT, wiT, kb, kib, vT, x, bo, gate, wo)


def _odd_kernel(x_ref, sc_ref, sh_ref, gate_ref, g_ref, win_ref, wm_ref, bias_ref, lng_ref, lnb_ref,
                wo_ref, fg_ref, y_ref, *rest, tm, lc, final, emit_v):
    if emit_v:
        v_ref, s_buf = rest
    else:
        (s_buf,) = rest
    x = x_ref[...]
    hb = _modulated_norm(x, g_ref[...], sc_ref[...], sh_ref[...]).astype(BF16)

    def proj(i):
        return jnp.dot(hb, win_ref[:, i * SGU_DIM:(i + 1) * SGU_DIM], preferred_element_type=F32)

    v = jax.nn.gelu(proj(1), approximate=True)
    mu = jnp.mean(v, axis=-1, keepdims=True)
    vc = v - mu
    vn = vc * lax.rsqrt(jnp.mean(vc * vc, axis=-1, keepdims=True) + EPS) * lng_ref[...] + lnb_ref[...]
    if emit_v:
        v_ref[...] = vn
    vb = vn.astype(BF16)
    for c in range(tm // lc):
        for g in range(N_SGU_GROUPS):
            cols = slice(g * SGU_GROUP_CH, (g + 1) * SGU_GROUP_CH)
            s_buf[c * lc:(c + 1) * lc, cols] = jnp.dot(
                wm_ref[g], vb[c * lc:(c + 1) * lc, cols], preferred_element_type=F32) + bias_ref[:, cols]
    u = jax.nn.gelu(proj(0), approximate=True)
    m = (u * s_buf[...] * _silu(proj(2))).astype(BF16)
    xn = x + gate_ref[...] * jnp.dot(m, wo_ref[...], preferred_element_type=F32)
    if final:
        xn = xn * lax.rsqrt(jnp.mean(xn * xn, axis=-1, keepdims=True) + EPS) * fg_ref[...]
    y_ref[...] = xn


def _odd_call(x, sc1p, sh, gate, g, win, wm, bias, lng, lnb, wo, fg, *, tm, final, emit_v):
    nb, t_len, d = x.shape
    lc = wm.shape[-1]
    kern = functools.partial(_odd_kernel, tm=tm, lc=lc, final=final, emit_v=emit_v)
    tok = pl.BlockSpec((None, tm, d), lambda b, t: (b, t, 0))
    per_b = pl.BlockSpec((None, 1, d), lambda b, t: (b, 0, 0))
    full = lambda a: pl.BlockSpec(a.shape, lambda b, t: (0,) * a.ndim)
    out_shape = [jax.ShapeDtypeStruct((nb, t_len, d), F32)]
    out_specs = [tok]
    if emit_v:
        out_shape.append(jax.ShapeDtypeStruct((nb, t_len, SGU_DIM), F32))
        out_specs.append(tok)
    res = pl.pallas_call(
        kern,
        grid=(nb, t_len // tm),
        in_specs=[tok, per_b, per_b, per_b, full(g), full(win), full(wm), full(bias), full(lng), full(lnb),
                  full(wo), full(fg)],
        out_specs=out_specs,
        out_shape=out_shape,
        scratch_shapes=[pltpu.VMEM((tm, SGU_DIM), F32)],
        compiler_params=pltpu.CompilerParams(
            dimension_semantics=("arbitrary", "arbitrary"), vmem_limit_bytes=VMEM_LIMIT_BYTES),
        name="odd_sgu",
    )(x, sc1p, sh, gate, g, win, wm, bias, lng, lnb, wo, fg)
    return (res[0], res[1]) if emit_v else (res[0], None)


def _pack_even_weights(w_in):
    ad, kv, idx = ATTN_DIM, N_KV * HEAD_DIM, N_IDX_HEADS * IDX_DIM
    cuts = [0]
    for s in (ad, kv, kv, ad, idx, IDX_DIM, N_IDX_HEADS, CONV_DIM, CONV_DIM, CONV_DIM, CONV_DIM):
        cuts.append(cuts[-1] + s)
    q, k, v, ga, qi, ki, wi, gb, gc, xin, gz = [w_in[:, cuts[i]:cuts[i + 1]] for i in range(11)]
    d = w_in.shape[0]
    wnn = jnp.concatenate([gb, gc, xin, gz, k, v, ki, jnp.zeros((d, NN_COLS - NN_KI - IDX_DIM), w_in.dtype)],
                          axis=1).astype(BF16)
    wnt = jnp.concatenate([q, ga, qi, v, wi, jnp.zeros((d, NT_ROWS - NT_WI - N_IDX_HEADS), w_in.dtype)],
                          axis=1).T.astype(BF16)
    return wnn, wnt


def _row(a):
    return a.reshape(1, -1)


def _trunk(x, mod, norm_g, ev_w, ev_conv_w, ev_w_out, od_w_in, od_wm, od_bias, od_ln_g, od_ln_b, od_w_out,
           final_g, cache, *, tm_even, tm_odd, tq):
    nb, t_len, d = x.shape
    depth = mod.shape[0]
    ks, vs, kis, convs, cvs = [], [], [], [], []
    for l in range(depth):
        shift = mod[l, :, None, 0:d]
        sc1p = 1.0 + mod[l, :, None, d:2 * d]
        gate = mod[l, :, None, 2 * d:3 * d]
        g = _row(norm_g[l])
        if l % 2 == 0:
            e = l // 2
            wnn, wnt = ev_w[e]
            if cache is None:
                hist = jnp.zeros((nb, CONV_W - 1, CONV_DIM), F32)
            else:
                hist = cache[3][e]
            (qT, gaT, qiT, vT, wiT, k, v, ki, kb, kib, bo, cst) = _even_in_call(
                x, sc1p, sh=shift, g=g, wnn=wnn, wnt=wnt, cw=ev_conv_w[e], hist=hist, tm=tm_even)
            if cache is None:
                past, n_keys = 0, t_len
            else:
                pk, pv, pki = cache[0][e], cache[1][e], cache[2][e]
                past = pk.shape[1]
                n_keys = past + t_len
                n_pad = -n_keys % KEY_TILE
                pkb = jnp.moveaxis(pk.astype(BF16), 2, 1)
                kb = jnp.pad(jnp.concatenate([pkb, kb], axis=2), ((0, 0), (0, 0), (0, n_pad), (0, 0)))
                kib = jnp.pad(jnp.concatenate([pki.astype(BF16), kib], axis=1), ((0, 0), (0, n_pad), (0, 0)))
                pvT = pv.astype(BF16).reshape(nb, past // KEY_TILE, KEY_TILE, N_KV * HEAD_DIM)
                pvT = jnp.swapaxes(pvT, 2, 3)
                vT_new = jnp.pad(vT.reshape(nb, N_KV * HEAD_DIM, t_len), ((0, 0), (0, 0), (0, n_pad)))
                vT = jnp.concatenate([pvT, vT_new[:, None]], axis=1)
            x = _attn_call(qT, gaT, qiT, wiT, kb, kib, vT, x, bo, gate, ev_w_out[e],
                           tq=tq, past=past, n_keys=n_keys)
            ks.append(k.reshape(nb, t_len, N_KV, HEAD_DIM))
            vs.append(v.reshape(nb, t_len, N_KV, HEAD_DIM))
            kis.append(ki)
            convs.append(cst)
        else:
            o = l // 2
            x, vrow = _odd_call(x, sc1p, shift, gate, g, od_w_in[o], od_wm[o], od_bias[o], _row(od_ln_g[o]),
                                _row(od_ln_b[o]), od_w_out[o], _row(final_g), tm=tm_odd,
                                final=(l == depth - 1), emit_v=cache is not None)
            cvs.append(vrow)
    new_c = None if cache is None else jnp.stack(cvs)
    return x, jnp.stack(ks), jnp.stack(vs), jnp.stack(kis), jnp.stack(convs), new_c


def _sgu_weights(od_ws, od_bs, lc):
    wm = jnp.tril(od_ws[:, :, :lc, :lc]).astype(BF16)
    bias = jnp.repeat(jnp.swapaxes(od_bs[:, :, :lc], 1, 2), SGU_GROUP_CH, axis=2)
    return wm, bias


def kernel(x_prompt, x_sample, cache_a_k, cache_a_v, cache_a_kidx, state_b_conv, c_prompt, c_sample,
           ada_w, ada_b, norm_g, ev_w_in, ev_conv_w, ev_w_out, od_w_in, od_ws, od_bs, od_ln_g, od_ln_b,
           od_w_out, final_g):
    nbp, tp, _ = x_prompt.shape
    nbs, ts, _ = x_sample.shape
    n_even = ev_w_in.shape[0]

    mod = _ada_call(jnp.concatenate([c_prompt, c_sample], axis=0), ada_w, ada_b)
    ev_w = [_pack_even_weights(ev_w_in[e]) for e in range(n_even)]
    ev_wo = ev_w_out.astype(BF16)
    od_wi = od_w_in.astype(BF16)
    od_wo = od_w_out.astype(BF16)
    past_len = cache_a_k.shape[2]
    cache = (cache_a_k.reshape(n_even, nbs, past_len, N_KV, HEAD_DIM),
             cache_a_v.reshape(n_even, nbs, past_len, N_KV * HEAD_DIM),
             cache_a_kidx, state_b_conv)

    def run(x, mod_part, cache, tm_even, tm_odd, tq):
        t_len = x.shape[1]
        lc = min(SGU_CHUNK, t_len)
        wm, bias = _sgu_weights(od_ws, od_bs, lc)
        return _trunk(x, mod_part, norm_g, ev_w, ev_conv_w, ev_wo, od_wi, wm, bias, od_ln_g, od_ln_b, od_wo,
                      final_g, cache, tm_even=tm_even, tm_odd=tm_odd, tq=tq)

    y_p, k_p, v_p, ki_p, conv_p, _ = run(x_prompt, mod[:, :nbp], None, 512, 512, 128)
    y_s, k_s, v_s, ki_s, conv_s, cv_s = run(x_sample, mod[:, nbp:], cache, ts, ts, ts)
    return (y_p, y_s, k_p, v_p, ki_p, conv_p, k_s, v_s, ki_s, conv_s, cv_s)
```

```python
import functools
import math

import jax
import jax.numpy as jnp
from jax import lax
from jax.experimental import pallas as pl
from jax.experimental.pallas import tpu as pltpu

F32 = jnp.float32
BF16 = jnp.bfloat16

D_MODEL = 1024
EPS = 1e-6
CHUNK_SHIFT = 6
ATTN_DIM = 512
N_HEADS = 8
HEAD_DIM = 64
N_KV = 2
GQA_REP = N_HEADS // N_KV
N_IDX_HEADS = 8
IDX_DIM = 64
TOPK_MAX = 256
CONV_DIM = 512
CONV_W = 3
SGU_DIM = 1024
SGU_CHUNK = 128
N_SGU_GROUPS = 8
SGU_GROUP_CH = SGU_DIM // N_SGU_GROUPS

SUBLANES = 8
LANES = 128
BF16_ROWS = 16
VMEM_LIMIT_BYTES = 52 * 1024 * 1024

BISECT_STEPS = 14
NEG_SCORE = -3.0e38
POS_SCORE = 3.0e38
NEG_BIAS = -1.0e30
LOG2E = 1.4426950408889634

V_ROWS = HEAD_DIM + BF16_ROWS

NT_Q, NT_GA, NT_QI, NT_V, NT_WI, NT_ROWS = 0, 512, 1024, 1536, 1664, 1680
NN_GB, NN_GC, NN_XIN, NN_GZ, NN_K, NN_V, NN_KI, NN_COLS = 0, 512, 1024, 1536, 2048, 2176, 2304, 2432


def _silu(x):
    return x / (1.0 + jnp.exp(-x))


def _fold_rows(a, op):
    r, c = a.shape
    n = r // SUBLANES
    if n % 4 == 0 and n >= 8:
        return op(op(a.reshape(4, n // 4, SUBLANES, c), axis=1), axis=0)
    return op(a.reshape(n, SUBLANES, c), axis=0)


def _modulated_norm(x, g, sc1p, sh):
    ms = jnp.mean(x * x, axis=-1, keepdims=True)
    return (x * lax.rsqrt(ms + EPS) * g) * sc1p + sh


def _ada_kernel(c_ref, w_ref, b_ref, o_ref):
    cs = _silu(c_ref[...])
    o_ref[...] = jnp.dot(cs, w_ref[...], preferred_element_type=F32,
                         precision=lax.Precision.HIGHEST) + b_ref[...]


def _ada_call(c_all, ada_w, ada_b):
    depth, d, n3 = ada_w.shape
    nb = c_all.shape[0]
    bn = 1024
    return pl.pallas_call(
        _ada_kernel,
        grid=(depth, n3 // bn),
        in_specs=[
            pl.BlockSpec((nb, d), lambda l, n: (0, 0)),
            pl.BlockSpec((None, d, bn), lambda l, n: (l, 0, n)),
            pl.BlockSpec((None, 1, bn), lambda l, n: (l, 0, n)),
        ],
        out_specs=pl.BlockSpec((None, nb, bn), lambda l, n: (l, 0, n)),
        out_shape=jax.ShapeDtypeStruct((depth, nb, n3), F32),
        compiler_params=pltpu.CompilerParams(
            dimension_semantics=("arbitrary", "arbitrary"), vmem_limit_bytes=VMEM_LIMIT_BYTES),
        name="ada_mod",
    )(c_all, ada_w, ada_b.reshape(depth, 1, n3))


def _even_in_kernel(x_ref, sc_ref, sh_ref, g_ref, wnn_ref, wnt_ref, cw_ref, hist_ref,
                    qT_ref, gaT_ref, qiT_ref, vT_ref, wiT_ref, k_ref, v_ref, ki_ref,
                    kb_ref, kib_ref, bo_ref, cst_ref, zbuf, *, tm, vc):
    t = pl.program_id(1)
    hb = _modulated_norm(x_ref[...], g_ref[...], sc_ref[...], sh_ref[...]).astype(BF16)

    def nn(c0, c1):
        return jnp.dot(hb, wnn_ref[:, c0:c1], preferred_element_type=F32)

    def nt(r0, r1):
        return lax.dot_general(wnt_ref[r0:r1, :], hb, (((1,), (1,)), ((), ())),
                               preferred_element_type=F32)

    z = nn(NN_GC, NN_XIN) * nn(NN_XIN, NN_GZ)

    @pl.when(t == 0)
    def _():
        zbuf[0:SUBLANES, :] = jnp.zeros((SUBLANES, CONV_DIM), F32)
        zbuf[SUBLANES - 2:SUBLANES, :] = hist_ref[...]

    zbuf[SUBLANES:SUBLANES + tm, :] = z
    z1 = zbuf[SUBLANES - 1:SUBLANES - 1 + tm, :]
    z2 = zbuf[SUBLANES - 2:SUBLANES - 2 + tm, :]
    y = cw_ref[2:3, :] * z + cw_ref[1:2, :] * z1 + cw_ref[0:1, :] * z2
    last2 = zbuf[SUBLANES + tm - 2:SUBLANES + tm, :]
    zbuf[SUBLANES - 2:SUBLANES, :] = last2
    cst_ref[...] = last2
    bo_ref[...] = (nn(NN_GB, NN_GC) * y * _silu(nn(NN_GZ, NN_K))).astype(BF16)

    kvk = nn(NN_K, NN_COLS)
    k = kvk[:, 0:128]
    k_ref[...] = k
    v_ref[...] = kvk[:, 128:256]
    ki = kvk[:, 256:256 + IDX_DIM]
    ki_ref[...] = ki
    kib_ref[...] = ki.astype(BF16)
    for g in range(N_KV):
        kb_ref[g] = k[:, g * HEAD_DIM:(g + 1) * HEAD_DIM].astype(BF16)

    qT_ref[...] = (nt(NT_Q, NT_GA) * (HEAD_DIM ** -0.5 * LOG2E)).astype(BF16)
    gaT_ref[...] = _silu(nt(NT_GA, NT_QI))
    qiT_ref[...] = nt(NT_QI, NT_V).astype(BF16)
    vT = nt(NT_V, NT_WI).astype(BF16)
    ones = jnp.ones((BF16_ROWS, vc), BF16)
    for c in range(tm // vc):
        for g in range(N_KV):
            vT_ref[c, g * V_ROWS:g * V_ROWS + HEAD_DIM, :] = vT[g * HEAD_DIM:(g + 1) * HEAD_DIM, c * vc:(c + 1) * vc]
            vT_ref[c, g * V_ROWS + HEAD_DIM:(g + 1) * V_ROWS, :] = ones
    wiT_ref[...] = nt(NT_WI, NT_ROWS)[0:N_IDX_HEADS, :]


def _even_in_call(x, sc1p, sh, g, wnn, wnt, cw, hist, *, tm, vc):
    nb, t_len, d = x.shape
    nt_tiles = t_len // tm
    kern = functools.partial(_even_in_kernel, tm=tm, vc=vc)
    tok = lambda w: pl.BlockSpec((None, tm, w), lambda b, t: (b, t, 0))
    tokT = lambda r: pl.BlockSpec((None, r, tm), lambda b, t: (b, 0, t))
    per_b = lambda r, w: pl.BlockSpec((None, r, w), lambda b, t: (b, 0, 0))
    full2 = lambda a: pl.BlockSpec(a.shape, lambda b, t: (0, 0))
    out_shape = (
        jax.ShapeDtypeStruct((nb, ATTN_DIM, t_len), BF16),
        jax.ShapeDtypeStruct((nb, ATTN_DIM, t_len), F32),
        jax.ShapeDtypeStruct((nb, N_IDX_HEADS * IDX_DIM, t_len), BF16),
        jax.ShapeDtypeStruct((nb, t_len // vc, N_KV * V_ROWS, vc), BF16),
        jax.ShapeDtypeStruct((nb, N_IDX_HEADS, t_len), F32),
        jax.ShapeDtypeStruct((nb, t_len, N_KV * HEAD_DIM), F32),
        jax.ShapeDtypeStruct((nb, t_len, N_KV * HEAD_DIM), F32),
        jax.ShapeDtypeStruct((nb, t_len, IDX_DIM), F32),
        jax.ShapeDtypeStruct((nb, N_KV, t_len, HEAD_DIM), BF16),
        jax.ShapeDtypeStruct((nb, t_len, IDX_DIM), BF16),
        jax.ShapeDtypeStruct((nb, t_len, CONV_DIM), BF16),
        jax.ShapeDtypeStruct((nb, CONV_W - 1, CONV_DIM), F32),
    )
    out_specs = (
        tokT(ATTN_DIM), tokT(ATTN_DIM), tokT(N_IDX_HEADS * IDX_DIM),
        pl.BlockSpec((None, tm // vc, N_KV * V_ROWS, vc), lambda b, t: (b, t, 0, 0)),
        tokT(N_IDX_HEADS),
        tok(N_KV * HEAD_DIM), tok(N_KV * HEAD_DIM), tok(IDX_DIM),
        pl.BlockSpec((None, N_KV, tm, HEAD_DIM), lambda b, t: (b, 0, t, 0)),
        tok(IDX_DIM), tok(CONV_DIM), per_b(CONV_W - 1, CONV_DIM),
    )
    return pl.pallas_call(
        kern,
        grid=(nb, nt_tiles),
        in_specs=[tok(d), per_b(1, d), per_b(1, d), full2(g), full2(wnn), full2(wnt), full2(cw),
                  per_b(CONV_W - 1, CONV_DIM)],
        out_specs=out_specs,
        out_shape=out_shape,
        scratch_shapes=[pltpu.VMEM((tm + SUBLANES, CONV_DIM), F32)],
        compiler_params=pltpu.CompilerParams(
            dimension_semantics=("arbitrary", "arbitrary"), vmem_limit_bytes=VMEM_LIMIT_BYTES),
        name="even_in",
    )(x, sc1p, sh, g, wnn, wnt, cw, hist)


def _attn_block(jb, refs, *, tq, ks, past, n_keys, topk):
    (qT_ref, gaT_ref, qiT_ref, wiT_ref, kb_ref, kib_ref, vT_ref, x_ref, bo_ref, gate_ref, wo_ref,
     y_ref, s_ref, lg_ref, acc_ref) = refs
    q0 = past + jb * tq
    limit = min((((q0 + tq - 1) >> CHUNK_SHIFT) + 1) << CHUNK_SHIFT, n_keys)
    n_st = -(-limit // ks)
    n_full = ((q0 >> CHUNK_SHIFT) << CHUNK_SHIFT) // ks
    n_rows = n_st * ks
    kf = float(topk)
    qpos = q0 + lax.broadcasted_iota(jnp.int32, (1, tq), 1)
    qchunk = qpos >> CHUNK_SHIFT
    row_iota = lax.broadcasted_iota(jnp.int32, (ks, tq), 0)

    def tile_off(st):
        return pl.multiple_of(st * ks, ks)

    def score_tile(off):
        kib = kib_ref[pl.ds(off, ks), :]
        s = None
        for h in range(N_IDX_HEADS):
            rel = jnp.dot(kib, qiT_ref[h * IDX_DIM:(h + 1) * IDX_DIM, :], preferred_element_type=F32)
            term = wiT_ref[h:h + 1, :] * jnp.maximum(rel, 0.0)
            s = term if s is None else s + term
        return s

    def full_body(st, carry):
        mn, mx = carry
        off = tile_off(st)
        s = score_tile(off)
        s_ref[pl.ds(off, ks), :] = s
        return jnp.minimum(mn, _fold_rows(s, jnp.min)), jnp.maximum(mx, _fold_rows(s, jnp.max))

    mn, mx = lax.fori_loop(
        0, n_full, full_body,
        (jnp.full((SUBLANES, tq), POS_SCORE, F32), jnp.full((SUBLANES, tq), NEG_SCORE, F32)), unroll=2)
    for st in range(n_full, n_st):
        off = st * ks
        s = score_tile(off)
        spos = row_iota + off
        adm = ((spos >> CHUNK_SHIFT) <= qchunk) & (spos < n_keys)
        s_lo = jnp.where(adm, s, NEG_SCORE)
        s_ref[off:off + ks, :] = s_lo
        mn = jnp.minimum(mn, _fold_rows(jnp.where(adm, s, POS_SCORE), jnp.min))
        mx = jnp.maximum(mx, _fold_rows(s_lo, jnp.max))
    lo0 = jnp.min(mn, axis=0, keepdims=True)
    mx0 = jnp.max(mx, axis=0, keepdims=True)
    hi0 = mx0 + jnp.maximum(jnp.abs(mx0), 1e-30) * (2.0 ** -10)
    n_adm = jnp.minimum((qchunk + 1) << CHUNK_SHIFT, n_keys).astype(F32)

    def tiles():
        for st in range(n_st):
            yield st * ks, s_ref[st * ks:(st + 1) * ks, :]

    def count_where(pred):
        tot = None
        for off, s in tiles():
            c = _fold_rows(jnp.where(pred(off, s), 1.0, 0.0), jnp.sum)
            tot = c if tot is None else tot + c
        return jnp.sum(tot, axis=0, keepdims=True)

    def bisect_body(_, carry):
        lo, hi, c_lo, c_hi = carry
        mid = 0.5 * (lo + hi)
        c = count_where(lambda off, s: s >= mid)
        ge = c >= kf
        return (jnp.where(ge, mid, lo), jnp.where(ge, hi, mid),
                jnp.where(ge, c, c_lo), jnp.where(ge, c_hi, c))

    lo, hi, c_lo, c_hi = lax.fori_loop(
        0, BISECT_STEPS, bisect_body, (lo0, hi0, n_adm, jnp.zeros((1, tq), F32)))

    def peel_body(carry):
        lo, hi, c_lo, c_hi, done, _ = carry
        top = None
        for off, s in tiles():
            m = _fold_rows(jnp.where(s < hi, s, NEG_SCORE), jnp.max)
            top = m if top is None else jnp.maximum(top, m)
        top = jnp.max(top, axis=0, keepdims=True)
        c = count_where(lambda off, s: s >= top)
        live = done < 0.5
        fin = live & (c >= kf)
        down = live & (c < kf)
        lo = jnp.where(fin, top, lo)
        c_lo = jnp.where(fin, c, c_lo)
        hi = jnp.where(down, top, hi)
        c_hi = jnp.where(down, c, c_hi)
        done = jnp.where(fin, 1.0, done)
        return lo, hi, c_lo, c_hi, done, jnp.sum(1.0 - done)

    done0 = jnp.where(c_lo <= kf, 1.0, 0.0)
    lo, hi, c_lo, c_hi, _, _ = lax.while_loop(
        lambda carry: carry[5] > 0.0, peel_body, (lo, hi, c_lo, c_hi, done0, jnp.sum(1.0 - done0)))

    need = kf - c_hi
    n_j_steps = max(1, n_rows.bit_length())

    def tie_cut():
        def jbody(_, carry):
            jlo, jhi = carry
            jmid = (jlo + jhi) >> 1
            c = count_where(lambda off, s: (s >= lo) & (s < hi) & ((row_iota + off) <= jmid))
            ok = c >= need
            return jnp.where(ok, jlo, jmid), jnp.where(ok, jmid, jhi)

        _, jhi = lax.fori_loop(
            0, n_j_steps, jbody,
            (jnp.full((1, tq), -1, jnp.int32), jnp.full((1, tq), n_rows - 1, jnp.int32)))
        return jhi

    has_tie = jnp.max(c_lo) > kf
    jcut = lax.cond(has_tie, tie_cut, lambda: jnp.full((1, tq), n_rows - 1, jnp.int32))

    for off, s in tiles():
        sel = (s >= lo) & ((s >= hi) | ((row_iota + off) <= jcut))
        s_ref[off:off + ks, :] = jnp.where(sel, 0.0, NEG_BIAS)

    qTg = [jnp.concatenate(
        [qT_ref[(g * GQA_REP + r) * HEAD_DIM:(g * GQA_REP + r + 1) * HEAD_DIM, :] for r in range(GQA_REP)],
        axis=1) for g in range(N_KV)]

    def logit_body(st, m8s):
        off = tile_off(st)
        b = s_ref[pl.ds(off, ks), :]
        b4 = jnp.concatenate([b] * GQA_REP, axis=1)
        out = []
        for g in range(N_KV):
            lt = jnp.dot(kb_ref[g, pl.ds(off, ks), :], qTg[g], preferred_element_type=F32) + b4
            lg_ref[g, pl.ds(off, ks), :] = lt
            out.append(jnp.maximum(m8s[g], _fold_rows(lt, jnp.max)))
        return tuple(out)

    m8s = lax.fori_loop(0, n_st, logit_body,
                        tuple(jnp.full((SUBLANES, GQA_REP * tq), NEG_SCORE, F32) for _ in range(N_KV)),
                        unroll=2)
    ms = [jnp.max(m8, axis=0, keepdims=True) for m8 in m8s]
    acc_ref[...] = jnp.zeros(acc_ref.shape, F32)

    def pv_body(st, carry):
        off = tile_off(st)
        for g in range(N_KV):
            p = jnp.exp2((lg_ref[g, pl.ds(off, ks), :] - ms[g]).astype(BF16))
            acc_ref[g] += jnp.dot(vT_ref[st, g * V_ROWS:(g + 1) * V_ROWS, :], p, preferred_element_type=F32)
        return carry

    lax.fori_loop(0, n_st, pv_body, 0, unroll=2)
    o_rows = []
    for g in range(N_KV):
        o = acc_ref[g, 0:HEAD_DIM, :] / acc_ref[g, HEAD_DIM:HEAD_DIM + 1, :]
        for r in range(GQA_REP):
            o_rows.append(o[:, r * tq:(r + 1) * tq])
    oT = jnp.concatenate(o_rows, axis=0)

    a = (oT * gaT_ref[...]).T.astype(BF16)
    out = jnp.dot(a, wo_ref[0:ATTN_DIM, :], preferred_element_type=F32)
    out = out + jnp.dot(bo_ref[...], wo_ref[ATTN_DIM:ATTN_DIM + CONV_DIM, :], preferred_element_type=F32)
    y_ref[...] = x_ref[...] + gate_ref[...] * out


def _attn_kernel(*refs, nq, **static):
    if nq == 1:
        _attn_block(0, refs, **static)
        return
    j = pl.program_id(1)
    for jb in range(nq):
        pl.when(j == jb)(functools.partial(_attn_block, jb, refs, **static))


def _attn_call(qT, gaT, qiT, wiT, kb, kib, vT, x, bo, gate, wo, *, tq, past, n_keys):
    nb, t_len, d = x.shape
    n_keys_pad = kb.shape[2]
    ks = vT.shape[3]
    nq = t_len // tq
    topk = min(TOPK_MAX, n_keys // 4)
    kern = functools.partial(_attn_kernel, nq=nq, tq=tq, ks=ks, past=past, n_keys=n_keys, topk=topk)
    tokT = lambda r: pl.BlockSpec((None, r, tq), lambda b, j: (b, 0, j))
    tok = lambda w: pl.BlockSpec((None, tq, w), lambda b, j: (b, j, 0))
    return pl.pallas_call(
        kern,
        grid=(nb, nq),
        in_specs=[
            tokT(ATTN_DIM), tokT(ATTN_DIM), tokT(N_IDX_HEADS * IDX_DIM), tokT(N_IDX_HEADS),
            pl.BlockSpec((None, N_KV, n_keys_pad, HEAD_DIM), lambda b, j: (b, 0, 0, 0)),
            pl.BlockSpec((None, n_keys_pad, IDX_DIM), lambda b, j: (b, 0, 0)),
            pl.BlockSpec((None, n_keys_pad // ks, N_KV * V_ROWS, ks), lambda b, j: (b, 0, 0, 0)),
            tok(d), tok(CONV_DIM),
            pl.BlockSpec((None, 1, d), lambda b, j: (b, 0, 0)),
            pl.BlockSpec(wo.shape, lambda b, j: (0, 0)),
        ],
        out_specs=tok(d),
        out_shape=jax.ShapeDtypeStruct((nb, t_len, d), F32),
        scratch_shapes=[pltpu.VMEM((n_keys_pad, tq), F32),
                        pltpu.VMEM((N_KV, n_keys_pad, GQA_REP * tq), F32),
                        pltpu.VMEM((N_KV, V_ROWS, GQA_REP * tq), F32)],
        compiler_params=pltpu.CompilerParams(
            dimension_semantics=("arbitrary", "arbitrary"), vmem_limit_bytes=VMEM_LIMIT_BYTES),
        name="dsa_attn",
    )(qT, gaT, qiT, wiT, kb, kib, vT, x, bo, gate, wo)


def _odd_kernel(x_ref, sc_ref, sh_ref, gate_ref, g_ref, win_ref, wm_ref, bias_ref, lng_ref, lnb_ref,
                wo_ref, fg_ref, y_ref, *rest, tm, lc, final, emit_v):
    if emit_v:
        v_ref, s_buf = rest
    else:
        (s_buf,) = rest
    x = x_ref[...]
    hb = _modulated_norm(x, g_ref[...], sc_ref[...], sh_ref[...]).astype(BF16)

    def proj(i):
        return jnp.dot(hb, win_ref[:, i * SGU_DIM:(i + 1) * SGU_DIM], preferred_element_type=F32)

    v = jax.nn.gelu(proj(1), approximate=True)
    mu = jnp.mean(v, axis=-1, keepdims=True)
    vc = v - mu
    vn = vc * lax.rsqrt(jnp.mean(vc * vc, axis=-1, keepdims=True) + EPS) * lng_ref[...] + lnb_ref[...]
    if emit_v:
        v_ref[...] = vn
    vb = vn.astype(BF16)
    for c in range(tm // lc):
        for g in range(N_SGU_GROUPS):
            cols = slice(g * SGU_GROUP_CH, (g + 1) * SGU_GROUP_CH)
            s_buf[c * lc:(c + 1) * lc, cols] = jnp.dot(
                wm_ref[g], vb[c * lc:(c + 1) * lc, cols], preferred_element_type=F32) + bias_ref[:, cols]
    u = jax.nn.gelu(proj(0), approximate=True)
    m = (u * s_buf[...] * _silu(proj(2))).astype(BF16)
    xn = x + gate_ref[...] * jnp.dot(m, wo_ref[...], preferred_element_type=F32)
    if final:
        xn = xn * lax.rsqrt(jnp.mean(xn * xn, axis=-1, keepdims=True) + EPS) * fg_ref[...]
    y_ref[...] = xn


def _odd_call(x, sc1p, sh, gate, g, win, wm, bias, lng, lnb, wo, fg, *, tm, final, emit_v):
    nb, t_len, d = x.shape
    lc = wm.shape[-1]
    kern = functools.partial(_odd_kernel, tm=tm, lc=lc, final=final, emit_v=emit_v)
    tok = pl.BlockSpec((None, tm, d), lambda b, t: (b, t, 0))
    per_b = pl.BlockSpec((None, 1, d), lambda b, t: (b, 0, 0))
    full = lambda a: pl.BlockSpec(a.shape, lambda b, t: (0,) * a.ndim)
    out_shape = [jax.ShapeDtypeStruct((nb, t_len, d), F32)]
    out_specs = [tok]
    if emit_v:
        out_shape.append(jax.ShapeDtypeStruct((nb, t_len, SGU_DIM), F32))
        out_specs.append(tok)
    res = pl.pallas_call(
        kern,
        grid=(nb, t_len // tm),
        in_specs=[tok, per_b, per_b, per_b, full(g), full(win), full(wm), full(bias), full(lng), full(lnb),
                  full(wo), full(fg)],
        out_specs=out_specs,
        out_shape=out_shape,
        scratch_shapes=[pltpu.VMEM((tm, SGU_DIM), F32)],
        compiler_params=pltpu.CompilerParams(
            dimension_semantics=("arbitrary", "arbitrary"), vmem_limit_bytes=VMEM_LIMIT_BYTES),
        name="odd_sgu",
    )(x, sc1p, sh, gate, g, win, wm, bias, lng, lnb, wo, fg)
    return (res[0], res[1]) if emit_v else (res[0], None)


def _pack_even_weights(w_in):
    ad, kv, idx = ATTN_DIM, N_KV * HEAD_DIM, N_IDX_HEADS * IDX_DIM
    cuts = [0]
    for s in (ad, kv, kv, ad, idx, IDX_DIM, N_IDX_HEADS, CONV_DIM, CONV_DIM, CONV_DIM, CONV_DIM):
        cuts.append(cuts[-1] + s)
    q, k, v, ga, qi, ki, wi, gb, gc, xin, gz = [w_in[:, cuts[i]:cuts[i + 1]] for i in range(11)]
    d = w_in.shape[0]
    wnn = jnp.concatenate([gb, gc, xin, gz, k, v, ki, jnp.zeros((d, NN_COLS - NN_KI - IDX_DIM), w_in.dtype)],
                          axis=1).astype(BF16)
    wnt = jnp.concatenate([q, ga, qi, v, wi, jnp.zeros((d, NT_ROWS - NT_WI - N_IDX_HEADS), w_in.dtype)],
                          axis=1).T.astype(BF16)
    return wnn, wnt


def _row(a):
    return a.reshape(1, -1)


def _cached_value_tiles(pv, ks):
    nb, p_len, _ = pv.shape
    t = pv.astype(BF16).reshape(nb, p_len // ks, ks, N_KV, HEAD_DIM)
    t = jnp.transpose(t, (0, 1, 3, 4, 2))
    ones = jnp.ones((nb, p_len // ks, N_KV, BF16_ROWS, ks), BF16)
    return jnp.concatenate([t, ones], axis=3).reshape(nb, p_len // ks, N_KV * V_ROWS, ks)


def _trunk(x, mod, norm_g, ev_w, ev_conv_w, ev_w_out, od_w_in, od_wm, od_bias, od_ln_g, od_ln_b, od_w_out,
           final_g, cache, *, tm_even, tm_odd, tq, ks):
    nb, t_len, d = x.shape
    depth = mod.shape[0]
    ks_s, vs_s, kis, convs, cvs = [], [], [], [], []
    for l in range(depth):
        shift = mod[l, :, None, 0:d]
        sc1p = 1.0 + mod[l, :, None, d:2 * d]
        gate = mod[l, :, None, 2 * d:3 * d]
        g = _row(norm_g[l])
        if l % 2 == 0:
            e = l // 2
            wnn, wnt = ev_w[e]
            if cache is None:
                hist = jnp.zeros((nb, CONV_W - 1, CONV_DIM), F32)
            else:
                hist = cache[3][e]
            vc = min(ks, tm_even)
            (qT, gaT, qiT, vT, wiT, k, v, ki, kb, kib, bo, cst) = _even_in_call(
                x, sc1p, sh=shift, g=g, wnn=wnn, wnt=wnt, cw=ev_conv_w[e], hist=hist, tm=tm_even, vc=vc)
            if cache is None:
                past, n_keys = 0, t_len
            else:
                pk, pv, pki = cache[0][e], cache[1][e], cache[2][e]
                past = pk.shape[1]
                n_keys = past + t_len
                n_pad = -n_keys % ks
                pkb = jnp.moveaxis(pk.astype(BF16), 2, 1)
                kb = jnp.pad(jnp.concatenate([pkb, kb], axis=2), ((0, 0), (0, 0), (0, n_pad), (0, 0)))
                kib = jnp.pad(jnp.concatenate([pki.astype(BF16), kib], axis=1), ((0, 0), (0, n_pad), (0, 0)))
                vT_new = jnp.pad(vT, ((0, 0), (0, 0), (0, 0), (0, ks - vc)))
                vT = jnp.concatenate([_cached_value_tiles(pv, ks), vT_new], axis=1)
            x = _attn_call(qT, gaT, qiT, wiT, kb, kib, vT, x, bo, gate, ev_w_out[e],
                           tq=tq, past=past, n_keys=n_keys)
            ks_s.append(k.reshape(nb, t_len, N_KV, HEAD_DIM))
            vs_s.append(v.reshape(nb, t_len, N_KV, HEAD_DIM))
            kis.append(ki)
            convs.append(cst)
        else:
            o = l // 2
            x, vrow = _odd_call(x, sc1p, shift, gate, g, od_w_in[o], od_wm[o], od_bias[o], _row(od_ln_g[o]),
                                _row(od_ln_b[o]), od_w_out[o], _row(final_g), tm=tm_odd,
                                final=(l == depth - 1), emit_v=cache is not None)
            cvs.append(vrow)
    new_c = None if cache is None else jnp.stack(cvs)
    return x, jnp.stack(ks_s), jnp.stack(vs_s), jnp.stack(kis), jnp.stack(convs), new_c


def _sgu_weights(od_ws, od_bs, lc):
    wm = jnp.tril(od_ws[:, :, :lc, :lc]).astype(BF16)
    bias = jnp.repeat(jnp.swapaxes(od_bs[:, :, :lc], 1, 2), SGU_GROUP_CH, axis=2)
    return wm, bias


def kernel(x_prompt, x_sample, cache_a_k, cache_a_v, cache_a_kidx, state_b_conv, c_prompt, c_sample,
           ada_w, ada_b, norm_g, ev_w_in, ev_conv_w, ev_w_out, od_w_in, od_ws, od_bs, od_ln_g, od_ln_b,
           od_w_out, final_g):
    nbp, tp, _ = x_prompt.shape
    nbs, ts, _ = x_sample.shape
    n_even = ev_w_in.shape[0]

    mod = _ada_call(jnp.concatenate([c_prompt, c_sample], axis=0), ada_w, ada_b)
    ev_w = [_pack_even_weights(ev_w_in[e]) for e in range(n_even)]
    ev_wo = ev_w_out.astype(BF16)
    od_wi = od_w_in.astype(BF16)
    od_wo = od_w_out.astype(BF16)
    past_len = cache_a_k.shape[2]
    cache = (cache_a_k, cache_a_v.reshape(n_even, nbs, past_len, N_KV * HEAD_DIM), cache_a_kidx, state_b_conv)

    def run(x, mod_part, cache, tm_even, tm_odd, tq, ks):
        t_len = x.shape[1]
        lc = min(SGU_CHUNK, t_len)
        wm, bias = _sgu_weights(od_ws, od_bs, lc)
        return _trunk(x, mod_part, norm_g, ev_w, ev_conv_w, ev_wo, od_wi, wm, bias, od_ln_g, od_ln_b, od_wo,
                      final_g, cache, tm_even=tm_even, tm_odd=tm_odd, tq=tq, ks=ks)

    y_p, k_p, v_p, ki_p, conv_p, _ = run(x_prompt, mod[:, :nbp], None, 512, 512, 256, 256)
    y_s, k_s, v_s, ki_s, conv_s, cv_s = run(x_sample, mod[:, nbp:], cache, ts, ts, ts, 128)
    return (y_p, y_s, k_p, v_p, ki_p, conv_p, k_s, v_s, ki_s, conv_s, cv_s)
```

```python
import functools
import math

import jax
import jax.numpy as jnp
from jax import lax
from jax.experimental import pallas as pl
from jax.experimental.pallas import tpu as pltpu

F32 = jnp.float32
BF16 = jnp.bfloat16

D_MODEL = 1024
EPS = 1e-6
CHUNK_SHIFT = 6
ATTN_DIM = 512
N_HEADS = 8
HEAD_DIM = 64
N_KV = 2
GQA_REP = N_HEADS // N_KV
N_IDX_HEADS = 8
IDX_DIM = 64
TOPK_MAX = 256
CONV_DIM = 512
CONV_W = 3
SGU_DIM = 1024
SGU_CHUNK = 128
N_SGU_GROUPS = 8
SGU_GROUP_CH = SGU_DIM // N_SGU_GROUPS

SUBLANES = 8
LANES = 128
BF16_ROWS = 16
VMEM_LIMIT_BYTES = 52 * 1024 * 1024

BISECT_STEPS = 14
PV_CHUNK_TILES = 4
LOGIT_BOUND_SLACK = 1.01
MIN_SOFTMAX_SUM = 2.0 ** -100
NEG_SCORE = -3.0e38
POS_SCORE = 3.0e38
NEG_BIAS = -1.0e30
LOG2E = 1.4426950408889634

V_ROWS = HEAD_DIM + BF16_ROWS

NT_Q, NT_GA, NT_QI, NT_V, NT_WI, NT_ROWS = 0, 512, 1024, 1536, 1664, 1680
NN_GB, NN_GC, NN_XIN, NN_GZ, NN_K, NN_V, NN_KI, NN_COLS = 0, 512, 1024, 1536, 2048, 2176, 2304, 2432


def _silu(x):
    return x / (1.0 + jnp.exp(-x))


def _fold_rows(a, op):
    r, c = a.shape
    n = r // SUBLANES
    if n % 4 == 0 and n >= 8:
        return op(op(a.reshape(4, n // 4, SUBLANES, c), axis=1), axis=0)
    return op(a.reshape(n, SUBLANES, c), axis=0)


def _modulated_norm(x, g, sc1p, sh):
    ms = jnp.mean(x * x, axis=-1, keepdims=True)
    return (x * lax.rsqrt(ms + EPS) * g) * sc1p + sh


def _ada_kernel(c_ref, w_ref, b_ref, o_ref):
    cs = _silu(c_ref[...])
    o_ref[...] = jnp.dot(cs, w_ref[...], preferred_element_type=F32,
                         precision=lax.Precision.HIGHEST) + b_ref[...]


def _ada_call(c_all, ada_w, ada_b):
    depth, d, n3 = ada_w.shape
    nb = c_all.shape[0]
    bn = 1024
    return pl.pallas_call(
        _ada_kernel,
        grid=(depth, n3 // bn),
        in_specs=[
            pl.BlockSpec((nb, d), lambda l, n: (0, 0)),
            pl.BlockSpec((None, d, bn), lambda l, n: (l, 0, n)),
            pl.BlockSpec((None, 1, bn), lambda l, n: (l, 0, n)),
        ],
        out_specs=pl.BlockSpec((None, nb, bn), lambda l, n: (l, 0, n)),
        out_shape=jax.ShapeDtypeStruct((depth, nb, n3), F32),
        compiler_params=pltpu.CompilerParams(
            dimension_semantics=("arbitrary", "arbitrary"), vmem_limit_bytes=VMEM_LIMIT_BYTES),
        name="ada_mod",
    )(c_all, ada_w, ada_b.reshape(depth, 1, n3))


def _even_in_kernel(x_ref, sc_ref, sh_ref, g_ref, wnn_ref, wnt_ref, cw_ref, hist_ref,
                    qT_ref, gaT_ref, qiT_ref, vT_ref, wiT_ref, k_ref, v_ref, ki_ref,
                    kb_ref, kib_ref, bo_ref, cst_ref, zbuf, *, tm, vc):
    t = pl.program_id(1)
    hb = _modulated_norm(x_ref[...], g_ref[...], sc_ref[...], sh_ref[...]).astype(BF16)

    def nn(c0, c1):
        return jnp.dot(hb, wnn_ref[:, c0:c1], preferred_element_type=F32)

    def nt(r0, r1):
        return lax.dot_general(wnt_ref[r0:r1, :], hb, (((1,), (1,)), ((), ())),
                               preferred_element_type=F32)

    z = nn(NN_GC, NN_XIN) * nn(NN_XIN, NN_GZ)

    @pl.when(t == 0)
    def _():
        zbuf[0:SUBLANES, :] = jnp.zeros((SUBLANES, CONV_DIM), F32)
        zbuf[SUBLANES - 2:SUBLANES, :] = hist_ref[...]

    zbuf[SUBLANES:SUBLANES + tm, :] = z
    z1 = zbuf[SUBLANES - 1:SUBLANES - 1 + tm, :]
    z2 = zbuf[SUBLANES - 2:SUBLANES - 2 + tm, :]
    y = cw_ref[2:3, :] * z + cw_ref[1:2, :] * z1 + cw_ref[0:1, :] * z2
    last2 = zbuf[SUBLANES + tm - 2:SUBLANES + tm, :]
    zbuf[SUBLANES - 2:SUBLANES, :] = last2
    cst_ref[...] = last2
    bo_ref[...] = (nn(NN_GB, NN_GC) * y * _silu(nn(NN_GZ, NN_K))).astype(BF16)

    kvk = nn(NN_K, NN_COLS)
    k = kvk[:, 0:128]
    k_ref[...] = k
    v_ref[...] = kvk[:, 128:256]
    ki = kvk[:, 256:256 + IDX_DIM]
    ki_ref[...] = ki
    kib_ref[...] = ki.astype(BF16)
    lane = lax.broadcasted_iota(jnp.int32, (tm, LANES), 1)
    ones_col = jnp.where(lane == HEAD_DIM, 1.0, 0.0)
    kb_ref[0] = jnp.where(lane < HEAD_DIM, k, ones_col).astype(BF16)
    kb_ref[1] = jnp.where(lane < HEAD_DIM, pltpu.roll(k, HEAD_DIM, 1), ones_col).astype(BF16)

    qT_ref[...] = (nt(NT_Q, NT_GA) * (HEAD_DIM ** -0.5 * LOG2E)).astype(BF16)
    gaT_ref[...] = _silu(nt(NT_GA, NT_QI))
    qiT_ref[...] = nt(NT_QI, NT_V).astype(BF16)
    vT = nt(NT_V, NT_WI).astype(BF16)
    ones = jnp.ones((BF16_ROWS, vc), BF16)
    for c in range(tm // vc):
        for g in range(N_KV):
            vT_ref[c, g * V_ROWS:g * V_ROWS + HEAD_DIM, :] = vT[g * HEAD_DIM:(g + 1) * HEAD_DIM, c * vc:(c + 1) * vc]
            vT_ref[c, g * V_ROWS + HEAD_DIM:(g + 1) * V_ROWS, :] = ones
    wiT_ref[...] = nt(NT_WI, NT_ROWS)[0:N_IDX_HEADS, :]


def _even_in_call(x, sc1p, sh, g, wnn, wnt, cw, hist, *, tm, vc):
    nb, t_len, d = x.shape
    nt_tiles = t_len // tm
    kern = functools.partial(_even_in_kernel, tm=tm, vc=vc)
    tok = lambda w: pl.BlockSpec((None, tm, w), lambda b, t: (b, t, 0))
    tokT = lambda r: pl.BlockSpec((None, r, tm), lambda b, t: (b, 0, t))
    per_b = lambda r, w: pl.BlockSpec((None, r, w), lambda b, t: (b, 0, 0))
    full2 = lambda a: pl.BlockSpec(a.shape, lambda b, t: (0, 0))
    out_shape = (
        jax.ShapeDtypeStruct((nb, ATTN_DIM, t_len), BF16),
        jax.ShapeDtypeStruct((nb, ATTN_DIM, t_len), F32),
        jax.ShapeDtypeStruct((nb, N_IDX_HEADS * IDX_DIM, t_len), BF16),
        jax.ShapeDtypeStruct((nb, t_len // vc, N_KV * V_ROWS, vc), BF16),
        jax.ShapeDtypeStruct((nb, N_IDX_HEADS, t_len), F32),
        jax.ShapeDtypeStruct((nb, t_len, N_KV * HEAD_DIM), F32),
        jax.ShapeDtypeStruct((nb, t_len, N_KV * HEAD_DIM), F32),
        jax.ShapeDtypeStruct((nb, t_len, IDX_DIM), F32),
        jax.ShapeDtypeStruct((nb, N_KV, t_len, LANES), BF16),
        jax.ShapeDtypeStruct((nb, t_len, IDX_DIM), BF16),
        jax.ShapeDtypeStruct((nb, t_len, CONV_DIM), BF16),
        jax.ShapeDtypeStruct((nb, CONV_W - 1, CONV_DIM), F32),
    )
    out_specs = (
        tokT(ATTN_DIM), tokT(ATTN_DIM), tokT(N_IDX_HEADS * IDX_DIM),
        pl.BlockSpec((None, tm // vc, N_KV * V_ROWS, vc), lambda b, t: (b, t, 0, 0)),
        tokT(N_IDX_HEADS),
        tok(N_KV * HEAD_DIM), tok(N_KV * HEAD_DIM), tok(IDX_DIM),
        pl.BlockSpec((None, N_KV, tm, LANES), lambda b, t: (b, 0, t, 0)),
        tok(IDX_DIM), tok(CONV_DIM), per_b(CONV_W - 1, CONV_DIM),
    )
    return pl.pallas_call(
        kern,
        grid=(nb, nt_tiles),
        in_specs=[tok(d), per_b(1, d), per_b(1, d), full2(g), full2(wnn), full2(wnt), full2(cw),
                  per_b(CONV_W - 1, CONV_DIM)],
        out_specs=out_specs,
        out_shape=out_shape,
        scratch_shapes=[pltpu.VMEM((tm + SUBLANES, CONV_DIM), F32)],
        compiler_params=pltpu.CompilerParams(
            dimension_semantics=("arbitrary", "arbitrary"), vmem_limit_bytes=VMEM_LIMIT_BYTES),
        name="even_in",
    )(x, sc1p, sh, g, wnn, wnt, cw, hist)


def _attn_block(jb, refs, *, tq, ks, past, n_keys, topk):
    (qT_ref, gaT_ref, qiT_ref, wiT_ref, kb_ref, kib_ref, vT_ref, x_ref, bo_ref, gate_ref, wo_ref,
     y_ref, s_ref, lg_ref, pb_ref, acc_ref, kn_ref) = refs
    q0 = past + jb * tq
    limit = min((((q0 + tq - 1) >> CHUNK_SHIFT) + 1) << CHUNK_SHIFT, n_keys)
    n_st = -(-limit // ks)
    n_full = ((q0 >> CHUNK_SHIFT) << CHUNK_SHIFT) // ks
    n_rows = n_st * ks
    kf = float(topk)
    qpos = q0 + lax.broadcasted_iota(jnp.int32, (1, tq), 1)
    qchunk = qpos >> CHUNK_SHIFT
    row_iota = lax.broadcasted_iota(jnp.int32, (ks, tq), 0)

    def tile_off(st):
        return pl.multiple_of(st * ks, ks)

    def score_tile(off):
        kib = kib_ref[pl.ds(off, ks), :]
        s = None
        for h in range(N_IDX_HEADS):
            rel = jnp.dot(kib, qiT_ref[h * IDX_DIM:(h + 1) * IDX_DIM, :], preferred_element_type=F32)
            term = wiT_ref[h:h + 1, :] * jnp.maximum(rel, 0.0)
            s = term if s is None else s + term
        return s

    def full_body(st, carry):
        mn, mx = carry
        off = tile_off(st)
        s = score_tile(off)
        s_ref[pl.ds(off, ks), :] = s
        return jnp.minimum(mn, _fold_rows(s, jnp.min)), jnp.maximum(mx, _fold_rows(s, jnp.max))

    mn, mx = lax.fori_loop(
        0, n_full, full_body,
        (jnp.full((SUBLANES, tq), POS_SCORE, F32), jnp.full((SUBLANES, tq), NEG_SCORE, F32)), unroll=2)
    for st in range(n_full, n_st):
        off = st * ks
        s = score_tile(off)
        spos = row_iota + off
        adm = ((spos >> CHUNK_SHIFT) <= qchunk) & (spos < n_keys)
        s_lo = jnp.where(adm, s, NEG_SCORE)
        s_ref[off:off + ks, :] = s_lo
        mn = jnp.minimum(mn, _fold_rows(jnp.where(adm, s, POS_SCORE), jnp.min))
        mx = jnp.maximum(mx, _fold_rows(s_lo, jnp.max))
    lo0 = jnp.min(mn, axis=0, keepdims=True)
    mx0 = jnp.max(mx, axis=0, keepdims=True)
    hi0 = mx0 + jnp.maximum(jnp.abs(mx0), 1e-30) * (2.0 ** -10)
    n_adm = jnp.minimum((qchunk + 1) << CHUNK_SHIFT, n_keys).astype(F32)

    def tiles():
        for st in range(n_st):
            yield st * ks, s_ref[st * ks:(st + 1) * ks, :]

    def count_where(pred):
        tot = None
        for off, s in tiles():
            c = _fold_rows(jnp.where(pred(off, s), 1.0, 0.0), jnp.sum)
            tot = c if tot is None else tot + c
        return jnp.sum(tot, axis=0, keepdims=True)

    def bisect_step(carry):
        lo, hi, c_lo, c_hi = carry
        mid = 0.5 * (lo + hi)
        c = count_where(lambda off, s: s >= mid)
        ge = c >= kf
        return (jnp.where(ge, mid, lo), jnp.where(ge, hi, mid),
                jnp.where(ge, c, c_lo), jnp.where(ge, c_hi, c))

    carry = bisect_step((lo0, hi0, n_adm, jnp.zeros((1, tq), F32)))
    y_ref[...] = jnp.dot(bo_ref[...], wo_ref[ATTN_DIM:ATTN_DIM + CONV_DIM, :], preferred_element_type=F32)
    lo, hi, c_lo, c_hi = lax.fori_loop(1, BISECT_STEPS, lambda _, c: bisect_step(c), carry)

    def peel_body(carry):
        lo, hi, c_lo, c_hi, done, _ = carry
        top = None
        for off, s in tiles():
            m = _fold_rows(jnp.where(s < hi, s, NEG_SCORE), jnp.max)
            top = m if top is None else jnp.maximum(top, m)
        top = jnp.max(top, axis=0, keepdims=True)
        c = count_where(lambda off, s: s >= top)
        live = done < 0.5
        fin = live & (c >= kf)
        down = live & (c < kf)
        lo = jnp.where(fin, top, lo)
        c_lo = jnp.where(fin, c, c_lo)
        hi = jnp.where(down, top, hi)
        c_hi = jnp.where(down, c, c_hi)
        done = jnp.where(fin, 1.0, done)
        return lo, hi, c_lo, c_hi, done, jnp.sum(1.0 - done)

    done0 = jnp.where(c_lo <= kf, 1.0, 0.0)
    lo, hi, c_lo, c_hi, _, _ = lax.while_loop(
        lambda carry: carry[5] > 0.0, peel_body, (lo, hi, c_lo, c_hi, done0, jnp.sum(1.0 - done0)))

    need = kf - c_hi
    n_j_steps = max(1, n_rows.bit_length())

    def tie_cut():
        def jbody(_, carry):
            jlo, jhi = carry
            jmid = (jlo + jhi) >> 1
            c = count_where(lambda off, s: (s >= lo) & (s < hi) & ((row_iota + off) <= jmid))
            ok = c >= need
            return jnp.where(ok, jlo, jmid), jnp.where(ok, jmid, jhi)

        _, jhi = lax.fori_loop(
            0, n_j_steps, jbody,
            (jnp.full((1, tq), -1, jnp.int32), jnp.full((1, tq), n_rows - 1, jnp.int32)))
        return jhi

    has_tie = jnp.max(c_lo) > kf
    jcut = lax.cond(has_tie, tie_cut, lambda: jnp.full((1, tq), n_rows - 1, jnp.int32))

    qTg = [jnp.concatenate(
        [qT_ref[(g * GQA_REP + r) * HEAD_DIM:(g * GQA_REP + r + 1) * HEAD_DIM, :] for r in range(GQA_REP)],
        axis=1) for g in range(N_KV)]

    zero_rows = jnp.zeros((LANES - HEAD_DIM, GQA_REP * tq), BF16)

    def tile_bias(off):
        s = s_ref[pl.ds(off, ks), :]
        sel = (s >= lo) & ((s >= hi) | ((row_iota + off) <= jcut))
        b = jnp.where(sel, 0.0, NEG_BIAS)
        return jnp.concatenate([b] * GQA_REP, axis=1)

    q2 = qT_ref[...].astype(F32)
    q2 = jnp.sum((q2 * q2).reshape(N_HEADS, HEAD_DIM, tq), axis=1)
    shift_row = lax.broadcasted_iota(jnp.int32, (LANES - HEAD_DIM, GQA_REP * tq), 0) == 0
    q_shift, q_plain = [], []
    for g in range(N_KV):
        k2 = kn_ref[0, g:g + 1, :]
        for st in range(1, n_st):
            k2 = jnp.maximum(k2, kn_ref[st, g:g + 1, :])
        bound = jnp.sqrt(q2[g * GQA_REP:(g + 1) * GQA_REP, :] * k2) * LOGIT_BOUND_SLACK
        bound = jnp.concatenate([bound[r:r + 1, :] for r in range(GQA_REP)], axis=1)
        q_shift.append(jnp.concatenate([qTg[g], jnp.where(shift_row, -bound, 0.0).astype(BF16)], axis=0))
        q_plain.append(jnp.concatenate([qTg[g], zero_rows], axis=0))
    def weight_body(st, carry):
        off = tile_off(st)
        b4 = tile_bias(off)
        for g in range(N_KV):
            x = jnp.dot(kb_ref[g, pl.ds(off, ks), :], q_shift[g], preferred_element_type=F32) + b4
            pb_ref[g, pl.ds(off, ks), :] = jnp.exp2(x.astype(BF16))
        return carry

    lax.fori_loop(0, n_st, weight_body, 0, unroll=2)

    def weighted_values(p_of):
        st0 = 0
        accs = [None] * N_KV
        while st0 < n_st:
            n_ch = min(PV_CHUNK_TILES, n_st - st0)
            rows = slice(st0 * ks, (st0 + n_ch) * ks)
            for g in range(N_KV):
                vcat = jnp.concatenate(
                    [vT_ref[st0 + c, g * V_ROWS:(g + 1) * V_ROWS, :] for c in range(n_ch)], axis=1)
                d = jnp.dot(vcat, p_of(g, rows), preferred_element_type=F32)
                accs[g] = d if accs[g] is None else accs[g] + d
            st0 += n_ch
        for g in range(N_KV):
            acc_ref[g] = accs[g]

    weighted_values(lambda g, rows: pb_ref[g, rows, :])

    def exact_softmax():
        def logit_body(st, m8s):
            off = tile_off(st)
            b4 = tile_bias(off)
            out = []
            for g in range(N_KV):
                lt = jnp.dot(kb_ref[g, pl.ds(off, ks), :], q_plain[g], preferred_element_type=F32) + b4
                lg_ref[g, pl.ds(off, ks), :] = lt
                out.append(jnp.maximum(m8s[g], _fold_rows(lt, jnp.max)))
            return tuple(out)

        m8s = lax.fori_loop(0, n_st, logit_body,
                            tuple(jnp.full((SUBLANES, GQA_REP * tq), NEG_SCORE, F32) for _ in range(N_KV)))
        ms = [jnp.max(m8, axis=0, keepdims=True) for m8 in m8s]
        weighted_values(lambda g, rows: jnp.exp2((lg_ref[g, rows, :] - ms[g]).astype(BF16)))

    l_min = jnp.minimum(jnp.min(acc_ref[0, HEAD_DIM:HEAD_DIM + 1, :]), jnp.min(acc_ref[1, HEAD_DIM:HEAD_DIM + 1, :]))
    pl.when(jnp.logical_not(l_min > MIN_SOFTMAX_SUM))(exact_softmax)

    o_rows = []
    for g in range(N_KV):
        o = acc_ref[g, 0:HEAD_DIM, :] / acc_ref[g, HEAD_DIM:HEAD_DIM + 1, :]
        for r in range(GQA_REP):
            o_rows.append(o[:, r * tq:(r + 1) * tq])
    oT = jnp.concatenate(o_rows, axis=0)

    a = (oT * gaT_ref[...]).T.astype(BF16)
    out = jnp.dot(a, wo_ref[0:ATTN_DIM, :], preferred_element_type=F32) + y_ref[...]
    y_ref[...] = x_ref[...] + gate_ref[...] * out


def _key_norms(refs, *, tq, ks):
    kb_ref, kn_ref = refs[4], refs[-1]
    for st in range(kn_ref.shape[0]):
        for g in range(N_KV):
            kk = kb_ref[g, st * ks:(st + 1) * ks, :].astype(F32)
            k2 = jnp.max(jnp.sum(kk * kk, axis=-1, keepdims=True), axis=0, keepdims=True)
            kn_ref[st, g:g + 1, :] = jnp.broadcast_to(k2, (1, tq))


def _attn_kernel(*refs, nq, **static):
    norms = functools.partial(_key_norms, refs, tq=static["tq"], ks=static["ks"])
    if nq == 1:
        norms()
        _attn_block(0, refs, **static)
        return
    j = pl.program_id(1)
    pl.when(j == 0)(norms)
    for jb in range(nq):
        pl.when(j == jb)(functools.partial(_attn_block, jb, refs, **static))


def _attn_call(qT, gaT, qiT, wiT, kb, kib, vT, x, bo, gate, wo, *, tq, past, n_keys):
    nb, t_len, d = x.shape
    n_keys_pad = kb.shape[2]
    ks = vT.shape[3]
    nq = t_len // tq
    topk = min(TOPK_MAX, n_keys // 4)
    kern = functools.partial(_attn_kernel, nq=nq, tq=tq, ks=ks, past=past, n_keys=n_keys, topk=topk)
    tokT = lambda r: pl.BlockSpec((None, r, tq), lambda b, j: (b, 0, j))
    tok = lambda w: pl.BlockSpec((None, tq, w), lambda b, j: (b, j, 0))
    return pl.pallas_call(
        kern,
        grid=(nb, nq),
        in_specs=[
            tokT(ATTN_DIM), tokT(ATTN_DIM), tokT(N_IDX_HEADS * IDX_DIM), tokT(N_IDX_HEADS),
            pl.BlockSpec((None, N_KV, n_keys_pad, LANES), lambda b, j: (b, 0, 0, 0)),
            pl.BlockSpec((None, n_keys_pad, IDX_DIM), lambda b, j: (b, 0, 0)),
            pl.BlockSpec((None, n_keys_pad // ks, N_KV * V_ROWS, ks), lambda b, j: (b, 0, 0, 0)),
            tok(d), tok(CONV_DIM),
            pl.BlockSpec((None, 1, d), lambda b, j: (b, 0, 0)),
            pl.BlockSpec(wo.shape, lambda b, j: (0, 0)),
        ],
        out_specs=tok(d),
        out_shape=jax.ShapeDtypeStruct((nb, t_len, d), F32),
        scratch_shapes=[pltpu.VMEM((n_keys_pad, tq), F32),
                        pltpu.VMEM((N_KV, n_keys_pad, GQA_REP * tq), F32),
                        pltpu.VMEM((N_KV, n_keys_pad, GQA_REP * tq), BF16),
                        pltpu.VMEM((N_KV, V_ROWS, GQA_REP * tq), F32),
                        pltpu.VMEM((n_keys_pad // ks, SUBLANES, tq), F32)],
        compiler_params=pltpu.CompilerParams(
            dimension_semantics=("arbitrary", "arbitrary"), vmem_limit_bytes=VMEM_LIMIT_BYTES),
        name="dsa_attn",
    )(qT, gaT, qiT, wiT, kb, kib, vT, x, bo, gate, wo)


def _odd_kernel(x_ref, sc_ref, sh_ref, gate_ref, g_ref, win_ref, wm_ref, bias_ref, lng_ref, lnb_ref,
                wo_ref, fg_ref, y_ref, *rest, tm, lc, final, emit_v):
    if emit_v:
        v_ref, s_buf = rest
    else:
        (s_buf,) = rest
    x = x_ref[...]
    hb = _modulated_norm(x, g_ref[...], sc_ref[...], sh_ref[...]).astype(BF16)

    def proj(i):
        return jnp.dot(hb, win_ref[:, i * SGU_DIM:(i + 1) * SGU_DIM], preferred_element_type=F32)

    v = jax.nn.gelu(proj(1), approximate=True)
    mu = jnp.mean(v, axis=-1, keepdims=True)
    vc = v - mu
    vn = vc * lax.rsqrt(jnp.mean(vc * vc, axis=-1, keepdims=True) + EPS) * lng_ref[...] + lnb_ref[...]
    if emit_v:
        v_ref[...] = vn
    vb = vn.astype(BF16)
    for c in range(tm // lc):
        for g in range(N_SGU_GROUPS):
            cols = slice(g * SGU_GROUP_CH, (g + 1) * SGU_GROUP_CH)
            s_buf[c * lc:(c + 1) * lc, cols] = jnp.dot(
                wm_ref[g], vb[c * lc:(c + 1) * lc, cols], preferred_element_type=F32) + bias_ref[:, cols]
    u = jax.nn.gelu(proj(0), approximate=True)
    m = (u * s_buf[...] * _silu(proj(2))).astype(BF16)
    xn = x + gate_ref[...] * jnp.dot(m, wo_ref[...], preferred_element_type=F32)
    if final:
        xn = xn * lax.rsqrt(jnp.mean(xn * xn, axis=-1, keepdims=True) + EPS) * fg_ref[...]
    y_ref[...] = xn


def _odd_call(x, sc1p, sh, gate, g, win, wm, bias, lng, lnb, wo, fg, *, tm, final, emit_v):
    nb, t_len, d = x.shape
    lc = wm.shape[-1]
    kern = functools.partial(_odd_kernel, tm=tm, lc=lc, final=final, emit_v=emit_v)
    tok = pl.BlockSpec((None, tm, d), lambda b, t: (b, t, 0))
    per_b = pl.BlockSpec((None, 1, d), lambda b, t: (b, 0, 0))
    full = lambda a: pl.BlockSpec(a.shape, lambda b, t: (0,) * a.ndim)
    out_shape = [jax.ShapeDtypeStruct((nb, t_len, d), F32)]
    out_specs = [tok]
    if emit_v:
        out_shape.append(jax.ShapeDtypeStruct((nb, t_len, SGU_DIM), F32))
        out_specs.append(tok)
    res = pl.pallas_call(
        kern,
        grid=(nb, t_len // tm),
        in_specs=[tok, per_b, per_b, per_b, full(g), full(win), full(wm), full(bias), full(lng), full(lnb),
                  full(wo), full(fg)],
        out_specs=out_specs,
        out_shape=out_shape,
        scratch_shapes=[pltpu.VMEM((tm, SGU_DIM), F32)],
        compiler_params=pltpu.CompilerParams(
            dimension_semantics=("arbitrary", "arbitrary"), vmem_limit_bytes=VMEM_LIMIT_BYTES),
        name="odd_sgu",
    )(x, sc1p, sh, gate, g, win, wm, bias, lng, lnb, wo, fg)
    return (res[0], res[1]) if emit_v else (res[0], None)


def _pack_even_weights(w_in):
    ad, kv, idx = ATTN_DIM, N_KV * HEAD_DIM, N_IDX_HEADS * IDX_DIM
    cuts = [0]
    for s in (ad, kv, kv, ad, idx, IDX_DIM, N_IDX_HEADS, CONV_DIM, CONV_DIM, CONV_DIM, CONV_DIM):
        cuts.append(cuts[-1] + s)
    q, k, v, ga, qi, ki, wi, gb, gc, xin, gz = [w_in[:, cuts[i]:cuts[i + 1]] for i in range(11)]
    d = w_in.shape[0]
    wnn = jnp.concatenate([gb, gc, xin, gz, k, v, ki, jnp.zeros((d, NN_COLS - NN_KI - IDX_DIM), w_in.dtype)],
                          axis=1).astype(BF16)
    wnt = jnp.concatenate([q, ga, qi, v, wi, jnp.zeros((d, NT_ROWS - NT_WI - N_IDX_HEADS), w_in.dtype)],
                          axis=1).T.astype(BF16)
    return wnn, wnt


def _row(a):
    return a.reshape(1, -1)


def _cached_value_tiles(pv, ks):
    nb, p_len, _ = pv.shape
    t = pv.astype(BF16).reshape(nb, p_len // ks, ks, N_KV, HEAD_DIM)
    t = jnp.transpose(t, (0, 1, 3, 4, 2))
    ones = jnp.ones((nb, p_len // ks, N_KV, BF16_ROWS, ks), BF16)
    return jnp.concatenate([t, ones], axis=3).reshape(nb, p_len // ks, N_KV * V_ROWS, ks)


def _trunk(x, mod, norm_g, ev_w, ev_conv_w, ev_w_out, od_w_in, od_wm, od_bias, od_ln_g, od_ln_b, od_w_out,
           final_g, cache, *, tm_even, tm_odd, tq, ks):
    nb, t_len, d = x.shape
    depth = mod.shape[0]
    ks_s, vs_s, kis, convs, cvs = [], [], [], [], []
    for l in range(depth):
        shift = mod[l, :, None, 0:d]
        sc1p = 1.0 + mod[l, :, None, d:2 * d]
        gate = mod[l, :, None, 2 * d:3 * d]
        g = _row(norm_g[l])
        if l % 2 == 0:
            e = l // 2
            wnn, wnt = ev_w[e]
            if cache is None:
                hist = jnp.zeros((nb, CONV_W - 1, CONV_DIM), F32)
            else:
                hist = cache[3][e]
            vc = min(ks, tm_even)
            (qT, gaT, qiT, vT, wiT, k, v, ki, kb, kib, bo, cst) = _even_in_call(
                x, sc1p, sh=shift, g=g, wnn=wnn, wnt=wnt, cw=ev_conv_w[e], hist=hist, tm=tm_even, vc=vc)
            if cache is None:
                past, n_keys = 0, t_len
            else:
                pk, pv, pki = cache[0][e], cache[1][e], cache[2][e]
                past = pk.shape[1]
                n_keys = past + t_len
                n_pad = -n_keys % ks
                pkb = jnp.moveaxis(pk.astype(BF16), 2, 1)
                pkb = jnp.concatenate([pkb, jnp.ones(pkb.shape[:3] + (1,), BF16),
                                       jnp.zeros(pkb.shape[:3] + (LANES - HEAD_DIM - 1,), BF16)], axis=3)
                kb = jnp.pad(jnp.concatenate([pkb, kb], axis=2), ((0, 0), (0, 0), (0, n_pad), (0, 0)))
                kib = jnp.pad(jnp.concatenate([pki.astype(BF16), kib], axis=1), ((0, 0), (0, n_pad), (0, 0)))
                vT_new = jnp.pad(vT, ((0, 0), (0, 0), (0, 0), (0, ks - vc)))
                vT = jnp.concatenate([_cached_value_tiles(pv, ks), vT_new], axis=1)
            x = _attn_call(qT, gaT, qiT, wiT, kb, kib, vT, x, bo, gate, ev_w_out[e],
                           tq=tq, past=past, n_keys=n_keys)
            ks_s.append(k.reshape(nb, t_len, N_KV, HEAD_DIM))
            vs_s.append(v.reshape(nb, t_len, N_KV, HEAD_DIM))
            kis.append(ki)
            convs.append(cst)
        else:
            o = l // 2
            x, vrow = _odd_call(x, sc1p, shift, gate, g, od_w_in[o], od_wm[o], od_bias[o], _row(od_ln_g[o]),
                                _row(od_ln_b[o]), od_w_out[o], _row(final_g), tm=tm_odd,
                                final=(l == depth - 1), emit_v=cache is not None)
            cvs.append(vrow)
    new_c = None if cache is None else jnp.stack(cvs)
    return x, jnp.stack(ks_s), jnp.stack(vs_s), jnp.stack(kis), jnp.stack(convs), new_c


def _sgu_weights(od_ws, od_bs, lc):
    wm = jnp.tril(od_ws[:, :, :lc, :lc]).astype(BF16)
    bias = jnp.repeat(jnp.swapaxes(od_bs[:, :, :lc], 1, 2), SGU_GROUP_CH, axis=2)
    return wm, bias


def kernel(x_prompt, x_sample, cache_a_k, cache_a_v, cache_a_kidx, state_b_conv, c_prompt, c_sample,
           ada_w, ada_b, norm_g, ev_w_in, ev_conv_w, ev_w_out, od_w_in, od_ws, od_bs, od_ln_g, od_ln_b,
           od_w_out, final_g):
    nbp, tp, _ = x_prompt.shape
    nbs, ts, _ = x_sample.shape
    n_even = ev_w_in.shape[0]

    mod = _ada_call(jnp.concatenate([c_prompt, c_sample], axis=0), ada_w, ada_b)
    ev_w = [_pack_even_weights(ev_w_in[e]) for e in range(n_even)]
    ev_wo = ev_w_out.astype(BF16)
    od_wi = od_w_in.astype(BF16)
    od_wo = od_w_out.astype(BF16)
    past_len = cache_a_k.shape[2]
    cache = (cache_a_k, cache_a_v.reshape(n_even, nbs, past_len, N_KV * HEAD_DIM), cache_a_kidx, state_b_conv)

    def run(x, mod_part, cache, tm_even, tm_odd, tq, ks):
        t_len = x.shape[1]
        lc = min(SGU_CHUNK, t_len)
        wm, bias = _sgu_weights(od_ws, od_bs, lc)
        return _trunk(x, mod_part, norm_g, ev_w, ev_conv_w, ev_wo, od_wi, wm, bias, od_ln_g, od_ln_b, od_wo,
                      final_g, cache, tm_even=tm_even, tm_odd=tm_odd, tq=tq, ks=ks)

    y_p, k_p, v_p, ki_p, conv_p, _ = run(x_prompt, mod[:, :nbp], None, 512, 512, 256, 256)
    y_s, k_s, v_s, ki_s, conv_s, cv_s = run(x_sample, mod[:, nbp:], cache, ts, ts, ts, 128)
    return (y_p, y_s, k_p, v_p, ki_p, conv_p, k_s, v_s, ki_s, conv_s, cv_s)
```

```python
import functools
import math

import jax
import jax.numpy as jnp
from jax import lax
from jax.experimental import pallas as pl
from jax.experimental.pallas import tpu as pltpu

F32 = jnp.float32
BF16 = jnp.bfloat16

D_MODEL = 1024
EPS = 1e-6
CHUNK_SHIFT = 6
ATTN_DIM = 512
N_HEADS = 8
HEAD_DIM = 64
N_KV = 2
GQA_REP = N_HEADS // N_KV
N_IDX_HEADS = 8
IDX_DIM = 64
TOPK_MAX = 256
CONV_DIM = 512
CONV_W = 3
SGU_DIM = 1024
SGU_CHUNK = 128
N_SGU_GROUPS = 8
SGU_GROUP_CH = SGU_DIM // N_SGU_GROUPS

SUBLANES = 8
LANES = 128
BF16_ROWS = 16
VMEM_LIMIT_BYTES = 52 * 1024 * 1024

BISECT_STEPS = 14
NEG_SCORE = -3.0e38
POS_SCORE = 3.0e38
NEG_BIAS = -1.0e30
LOG2E = 1.4426950408889634

V_ROWS = HEAD_DIM + BF16_ROWS

NT_Q, NT_GA, NT_QI, NT_V, NT_WI, NT_ROWS = 0, 512, 1024, 1536, 1664, 1680
NN_GB, NN_GC, NN_XIN, NN_GZ, NN_K, NN_V, NN_KI, NN_COLS = 0, 512, 1024, 1536, 2048, 2176, 2304, 2432


def _silu(x):
    return x / (1.0 + jnp.exp(-x))


def _fold_rows(a, op):
    r, c = a.shape
    n = r // SUBLANES
    if n % 4 == 0 and n >= 8:
        return op(op(a.reshape(4, n // 4, SUBLANES, c), axis=1), axis=0)
    return op(a.reshape(n, SUBLANES, c), axis=0)


def _modulated_norm(x, g, sc1p, sh):
    ms = jnp.mean(x * x, axis=-1, keepdims=True)
    return (x * lax.rsqrt(ms + EPS) * g) * sc1p + sh


def _ada_kernel(c_ref, w_ref, b_ref, o_ref):
    cs = _silu(c_ref[...])
    o_ref[...] = jnp.dot(cs, w_ref[...], preferred_element_type=F32,
                         precision=lax.Precision.HIGHEST) + b_ref[...]


def _ada_call(c_all, ada_w, ada_b):
    depth, d, n3 = ada_w.shape
    nb = c_all.shape[0]
    bn = 1024
    return pl.pallas_call(
        _ada_kernel,
        grid=(depth, n3 // bn),
        in_specs=[
            pl.BlockSpec((nb, d), lambda l, n: (0, 0)),
            pl.BlockSpec((None, d, bn), lambda l, n: (l, 0, n)),
            pl.BlockSpec((None, 1, bn), lambda l, n: (l, 0, n)),
        ],
        out_specs=pl.BlockSpec((None, nb, bn), lambda l, n: (l, 0, n)),
        out_shape=jax.ShapeDtypeStruct((depth, nb, n3), F32),
        compiler_params=pltpu.CompilerParams(
            dimension_semantics=("arbitrary", "arbitrary"), vmem_limit_bytes=VMEM_LIMIT_BYTES),
        name="ada_mod",
    )(c_all, ada_w, ada_b.reshape(depth, 1, n3))


def _even_in_kernel(x_ref, sc_ref, sh_ref, g_ref, wnn_ref, wnt_ref, cw_ref, hist_ref,
                    qT_ref, gaT_ref, qiT_ref, vT_ref, wiT_ref, k_ref, v_ref, ki_ref,
                    kb_ref, kib_ref, bo_ref, cst_ref, zbuf, *, tm, vc):
    t = pl.program_id(1)
    hb = _modulated_norm(x_ref[...], g_ref[...], sc_ref[...], sh_ref[...]).astype(BF16)

    def nn(c0, c1):
        return jnp.dot(hb, wnn_ref[:, c0:c1], preferred_element_type=F32)

    def nt(r0, r1):
        return lax.dot_general(wnt_ref[r0:r1, :], hb, (((1,), (1,)), ((), ())),
                               preferred_element_type=F32)

    z = nn(NN_GC, NN_XIN) * nn(NN_XIN, NN_GZ)

    @pl.when(t == 0)
    def _():
        zbuf[0:SUBLANES, :] = jnp.zeros((SUBLANES, CONV_DIM), F32)
        zbuf[SUBLANES - 2:SUBLANES, :] = hist_ref[...]

    zbuf[SUBLANES:SUBLANES + tm, :] = z
    z1 = zbuf[SUBLANES - 1:SUBLANES - 1 + tm, :]
    z2 = zbuf[SUBLANES - 2:SUBLANES - 2 + tm, :]
    y = cw_ref[2:3, :] * z + cw_ref[1:2, :] * z1 + cw_ref[0:1, :] * z2
    last2 = zbuf[SUBLANES + tm - 2:SUBLANES + tm, :]
    zbuf[SUBLANES - 2:SUBLANES, :] = last2
    cst_ref[...] = last2
    bo_ref[...] = (nn(NN_GB, NN_GC) * y * _silu(nn(NN_GZ, NN_K))).astype(BF16)

    kvk = nn(NN_K, NN_COLS)
    k = kvk[:, 0:128]
    k_ref[...] = k
    v_ref[...] = kvk[:, 128:256]
    ki = kvk[:, 256:256 + IDX_DIM]
    ki_ref[...] = ki
    kib_ref[...] = ki.astype(BF16)
    for g in range(N_KV):
        kb_ref[g] = k[:, g * HEAD_DIM:(g + 1) * HEAD_DIM].astype(BF16)

    qT_ref[...] = (nt(NT_Q, NT_GA) * (HEAD_DIM ** -0.5 * LOG2E)).astype(BF16)
    gaT_ref[...] = _silu(nt(NT_GA, NT_QI))
    qiT_ref[...] = nt(NT_QI, NT_V).astype(BF16)
    vT = nt(NT_V, NT_WI).astype(BF16)
    ones = jnp.ones((BF16_ROWS, vc), BF16)
    for c in range(tm // vc):
        for g in range(N_KV):
            vT_ref[c, g * V_ROWS:g * V_ROWS + HEAD_DIM, :] = vT[g * HEAD_DIM:(g + 1) * HEAD_DIM, c * vc:(c + 1) * vc]
            vT_ref[c, g * V_ROWS + HEAD_DIM:(g + 1) * V_ROWS, :] = ones
    wiT_ref[...] = nt(NT_WI, NT_ROWS)[0:N_IDX_HEADS, :]


def _even_in_call(x, sc1p, sh, g, wnn, wnt, cw, hist, *, tm, vc):
    nb, t_len, d = x.shape
    nt_tiles = t_len // tm
    kern = functools.partial(_even_in_kernel, tm=tm, vc=vc)
    tok = lambda w: pl.BlockSpec((None, tm, w), lambda b, t: (b, t, 0))
    tokT = lambda r: pl.BlockSpec((None, r, tm), lambda b, t: (b, 0, t))
    per_b = lambda r, w: pl.BlockSpec((None, r, w), lambda b, t: (b, 0, 0))
    full2 = lambda a: pl.BlockSpec(a.shape, lambda b, t: (0, 0))
    out_shape = (
        jax.ShapeDtypeStruct((nb, ATTN_DIM, t_len), BF16),
        jax.ShapeDtypeStruct((nb, ATTN_DIM, t_len), F32),
        jax.ShapeDtypeStruct((nb, N_IDX_HEADS * IDX_DIM, t_len), BF16),
        jax.ShapeDtypeStruct((nb, t_len // vc, N_KV * V_ROWS, vc), BF16),
        jax.ShapeDtypeStruct((nb, N_IDX_HEADS, t_len), F32),
        jax.ShapeDtypeStruct((nb, t_len, N_KV * HEAD_DIM), F32),
        jax.ShapeDtypeStruct((nb, t_len, N_KV * HEAD_DIM), F32),
        jax.ShapeDtypeStruct((nb, t_len, IDX_DIM), F32),
        jax.ShapeDtypeStruct((nb, N_KV, t_len, HEAD_DIM), BF16),
        jax.ShapeDtypeStruct((nb, t_len, IDX_DIM), BF16),
        jax.ShapeDtypeStruct((nb, t_len, CONV_DIM), BF16),
        jax.ShapeDtypeStruct((nb, CONV_W - 1, CONV_DIM), F32),
    )
    out_specs = (
        tokT(ATTN_DIM), tokT(ATTN_DIM), tokT(N_IDX_HEADS * IDX_DIM),
        pl.BlockSpec((None, tm // vc, N_KV * V_ROWS, vc), lambda b, t: (b, t, 0, 0)),
        tokT(N_IDX_HEADS),
        tok(N_KV * HEAD_DIM), tok(N_KV * HEAD_DIM), tok(IDX_DIM),
        pl.BlockSpec((None, N_KV, tm, HEAD_DIM), lambda b, t: (b, 0, t, 0)),
        tok(IDX_DIM), tok(CONV_DIM), per_b(CONV_W - 1, CONV_DIM),
    )
    return pl.pallas_call(
        kern,
        grid=(nb, nt_tiles),
        in_specs=[tok(d), per_b(1, d), per_b(1, d), full2(g), full2(wnn), full2(wnt), full2(cw),
                  per_b(CONV_W - 1, CONV_DIM)],
        out_specs=out_specs,
        out_shape=out_shape,
        scratch_shapes=[pltpu.VMEM((tm + SUBLANES, CONV_DIM), F32)],
        compiler_params=pltpu.CompilerParams(
            dimension_semantics=("arbitrary", "arbitrary"), vmem_limit_bytes=VMEM_LIMIT_BYTES),
        name="even_in",
    )(x, sc1p, sh, g, wnn, wnt, cw, hist)


def _attn_block(jb, refs, *, tq, ks, past, n_keys, topk):
    (qT_ref, gaT_ref, qiT_ref, wiT_ref, kb_ref, kib_ref, vT_ref, x_ref, bo_ref, gate_ref, wo_ref,
     y_ref, s_ref, lg_ref, acc_ref) = refs
    q0 = past + jb * tq
    limit = min((((q0 + tq - 1) >> CHUNK_SHIFT) + 1) << CHUNK_SHIFT, n_keys)
    n_st = -(-limit // ks)
    n_full = ((q0 >> CHUNK_SHIFT) << CHUNK_SHIFT) // ks
    n_rows = n_st * ks
    kf = float(topk)
    qpos = q0 + lax.broadcasted_iota(jnp.int32, (1, tq), 1)
    qchunk = qpos >> CHUNK_SHIFT
    row_iota = lax.broadcasted_iota(jnp.int32, (ks, tq), 0)

    def tile_off(st):
        return pl.multiple_of(st * ks, ks)

    def score_tile(off):
        kib = kib_ref[pl.ds(off, ks), :]
        s = None
        for h in range(N_IDX_HEADS):
            rel = jnp.dot(kib, qiT_ref[h * IDX_DIM:(h + 1) * IDX_DIM, :], preferred_element_type=F32)
            term = wiT_ref[h:h + 1, :] * jnp.maximum(rel, 0.0)
            s = term if s is None else s + term
        return s

    def full_body(st, carry):
        mn, mx = carry
        off = tile_off(st)
        s = score_tile(off)
        s_ref[pl.ds(off, ks), :] = s
        return jnp.minimum(mn, _fold_rows(s, jnp.min)), jnp.maximum(mx, _fold_rows(s, jnp.max))

    mn, mx = lax.fori_loop(
        0, n_full, full_body,
        (jnp.full((SUBLANES, tq), POS_SCORE, F32), jnp.full((SUBLANES, tq), NEG_SCORE, F32)))
    for st in range(n_full, n_st):
        off = st * ks
        s = score_tile(off)
        spos = row_iota + off
        adm = ((spos >> CHUNK_SHIFT) <= qchunk) & (spos < n_keys)
        s_lo = jnp.where(adm, s, NEG_SCORE)
        s_ref[off:off + ks, :] = s_lo
        mn = jnp.minimum(mn, _fold_rows(jnp.where(adm, s, POS_SCORE), jnp.min))
        mx = jnp.maximum(mx, _fold_rows(s_lo, jnp.max))
    lo0 = jnp.min(mn, axis=0, keepdims=True)
    mx0 = jnp.max(mx, axis=0, keepdims=True)
    hi0 = mx0 + jnp.maximum(jnp.abs(mx0), 1e-30) * (2.0 ** -10)
    n_adm = jnp.minimum((qchunk + 1) << CHUNK_SHIFT, n_keys).astype(F32)

    def tiles():
        for st in range(n_st):
            yield st * ks, s_ref[st * ks:(st + 1) * ks, :]

    def count_where(pred):
        tot = None
        for off, s in tiles():
            c = _fold_rows(jnp.where(pred(off, s), 1.0, 0.0), jnp.sum)
            tot = c if tot is None else tot + c
        return jnp.sum(tot, axis=0, keepdims=True)

    def fold_tiles(body, init):
        def step(st, carry):
            off = tile_off(st)
            return body(off, s_ref[pl.ds(off, ks), :], carry)
        return lax.fori_loop(0, n_st, step, init)

    def count_rolled(pred):
        acc = fold_tiles(lambda off, s, a: a + _fold_rows(jnp.where(pred(off, s), 1.0, 0.0), jnp.sum),
                         jnp.zeros((SUBLANES, tq), F32))
        return jnp.sum(acc, axis=0, keepdims=True)

    def bisect_body(_, carry):
        lo, hi, c_lo, c_hi = carry
        mid = 0.5 * (lo + hi)
        c = count_where(lambda off, s: s >= mid)
        ge = c >= kf
        return (jnp.where(ge, mid, lo), jnp.where(ge, hi, mid),
                jnp.where(ge, c, c_lo), jnp.where(ge, c_hi, c))

    lo, hi, c_lo, c_hi = lax.fori_loop(
        0, BISECT_STEPS, bisect_body, (lo0, hi0, n_adm, jnp.zeros((1, tq), F32)))

    def peel_body(carry):
        lo, hi, c_lo, c_hi, done, _ = carry
        top = fold_tiles(
            lambda off, s, t: jnp.maximum(t, _fold_rows(jnp.where(s < hi, s, NEG_SCORE), jnp.max)),
            jnp.full((SUBLANES, tq), NEG_SCORE, F32))
        top = jnp.max(top, axis=0, keepdims=True)
        c = count_rolled(lambda off, s: s >= top)
        live = done < 0.5
        fin = live & (c >= kf)
        down = live & (c < kf)
        lo = jnp.where(fin, top, lo)
        c_lo = jnp.where(fin, c, c_lo)
        hi = jnp.where(down, top, hi)
        c_hi = jnp.where(down, c, c_hi)
        done = jnp.where(fin, 1.0, done)
        return lo, hi, c_lo, c_hi, done, jnp.sum(1.0 - done)

    done0 = jnp.where(c_lo <= kf, 1.0, 0.0)
    lo, hi, c_lo, c_hi, _, _ = lax.while_loop(
        lambda carry: carry[5] > 0.0, peel_body, (lo, hi, c_lo, c_hi, done0, jnp.sum(1.0 - done0)))

    need = kf - c_hi
    n_j_steps = max(1, n_rows.bit_length())

    def tie_cut():
        def jbody(_, carry):
            jlo, jhi = carry
            jmid = (jlo + jhi) >> 1
            c = count_rolled(lambda off, s: (s >= lo) & (s < hi) & ((row_iota + off) <= jmid))
            ok = c >= need
            return jnp.where(ok, jlo, jmid), jnp.where(ok, jmid, jhi)

        _, jhi = lax.fori_loop(
            0, n_j_steps, jbody,
            (jnp.full((1, tq), -1, jnp.int32), jnp.full((1, tq), n_rows - 1, jnp.int32)))
        return jhi

    has_tie = jnp.max(c_lo) > kf
    jcut = lax.cond(has_tie, tie_cut, lambda: jnp.full((1, tq), n_rows - 1, jnp.int32))

    def select_body(off, s, carry):
        sel = (s >= lo) & ((s >= hi) | ((row_iota + off) <= jcut))
        s_ref[pl.ds(off, ks), :] = jnp.where(sel, 0.0, NEG_BIAS)
        return carry

    fold_tiles(select_body, 0)

    qTg = [jnp.concatenate(
        [qT_ref[(g * GQA_REP + r) * HEAD_DIM:(g * GQA_REP + r + 1) * HEAD_DIM, :] for r in range(GQA_REP)],
        axis=1) for g in range(N_KV)]

    def logit_body(st, m8s):
        off = tile_off(st)
        b = s_ref[pl.ds(off, ks), :]
        b4 = jnp.concatenate([b] * GQA_REP, axis=1)
        out = []
        for g in range(N_KV):
            lt = jnp.dot(kb_ref[g, pl.ds(off, ks), :], qTg[g], preferred_element_type=F32) + b4
            lg_ref[g, pl.ds(off, ks), :] = lt
            out.append(jnp.maximum(m8s[g], _fold_rows(lt, jnp.max)))
        return tuple(out)

    m8s = lax.fori_loop(0, n_st, logit_body,
                        tuple(jnp.full((SUBLANES, GQA_REP * tq), NEG_SCORE, F32) for _ in range(N_KV)))
    ms = [jnp.max(m8, axis=0, keepdims=True) for m8 in m8s]
    acc_ref[...] = jnp.zeros(acc_ref.shape, F32)

    def pv_body(st, carry):
        off = tile_off(st)
        for g in range(N_KV):
            p = jnp.exp2((lg_ref[g, pl.ds(off, ks), :] - ms[g]).astype(BF16))
            acc_ref[g] += jnp.dot(vT_ref[st, g * V_ROWS:(g + 1) * V_ROWS, :], p, preferred_element_type=F32)
        return carry

    lax.fori_loop(0, n_st, pv_body, 0)
    o_rows = []
    for g in range(N_KV):
        o = acc_ref[g, 0:HEAD_DIM, :] / acc_ref[g, HEAD_DIM:HEAD_DIM + 1, :]
        for r in range(GQA_REP):
            o_rows.append(o[:, r * tq:(r + 1) * tq])
    oT = jnp.concatenate(o_rows, axis=0)

    a = (oT * gaT_ref[...]).T.astype(BF16)
    out = jnp.dot(a, wo_ref[0:ATTN_DIM, :], preferred_element_type=F32)
    out = out + jnp.dot(bo_ref[...], wo_ref[ATTN_DIM:ATTN_DIM + CONV_DIM, :], preferred_element_type=F32)
    y_ref[...] = x_ref[...] + gate_ref[...] * out


def _attn_kernel(*refs, nq, **static):
    if nq == 1:
        _attn_block(0, refs, **static)
        return
    j = pl.program_id(1)
    for jb in range(nq):
        pl.when(j == jb)(functools.partial(_attn_block, jb, refs, **static))


def _attn_call(qT, gaT, qiT, wiT, kb, kib, vT, x, bo, gate, wo, *, tq, past, n_keys):
    nb, t_len, d = x.shape
    n_keys_pad = kb.shape[2]
    ks = vT.shape[3]
    nq = t_len // tq
    topk = min(TOPK_MAX, n_keys // 4)
    kern = functools.partial(_attn_kernel, nq=nq, tq=tq, ks=ks, past=past, n_keys=n_keys, topk=topk)
    tokT = lambda r: pl.BlockSpec((None, r, tq), lambda b, j: (b, 0, j))
    tok = lambda w: pl.BlockSpec((None, tq, w), lambda b, j: (b, j, 0))
    return pl.pallas_call(
        kern,
        grid=(nb, nq),
        in_specs=[
            tokT(ATTN_DIM), tokT(ATTN_DIM), tokT(N_IDX_HEADS * IDX_DIM), tokT(N_IDX_HEADS),
            pl.BlockSpec((None, N_KV, n_keys_pad, HEAD_DIM), lambda b, j: (b, 0, 0, 0)),
            pl.BlockSpec((None, n_keys_pad, IDX_DIM), lambda b, j: (b, 0, 0)),
            pl.BlockSpec((None, n_keys_pad // ks, N_KV * V_ROWS, ks), lambda b, j: (b, 0, 0, 0)),
            tok(d), tok(CONV_DIM),
            pl.BlockSpec((None, 1, d), lambda b, j: (b, 0, 0)),
            pl.BlockSpec(wo.shape, lambda b, j: (0, 0)),
        ],
        out_specs=tok(d),
        out_shape=jax.ShapeDtypeStruct((nb, t_len, d), F32),
        scratch_shapes=[pltpu.VMEM((n_keys_pad, tq), F32),
                        pltpu.VMEM((N_KV, n_keys_pad, GQA_REP * tq), F32),
                        pltpu.VMEM((N_KV, V_ROWS, GQA_REP * tq), F32)],
        compiler_params=pltpu.CompilerParams(
            dimension_semantics=("arbitrary", "arbitrary"), vmem_limit_bytes=VMEM_LIMIT_BYTES),
        name="dsa_attn",
    )(qT, gaT, qiT, wiT, kb, kib, vT, x, bo, gate, wo)


def _odd_kernel(x_ref, sc_ref, sh_ref, gate_ref, g_ref, win_ref, wm_ref, bias_ref, lng_ref, lnb_ref,
                wo_ref, fg_ref, y_ref, *rest, tm, lc, final, emit_v):
    if emit_v:
        v_ref, s_buf = rest
    else:
        (s_buf,) = rest
    x = x_ref[...]
    hb = _modulated_norm(x, g_ref[...], sc_ref[...], sh_ref[...]).astype(BF16)

    def proj(i):
        return jnp.dot(hb, win_ref[:, i * SGU_DIM:(i + 1) * SGU_DIM], preferred_element_type=F32)

    v = jax.nn.gelu(proj(1), approximate=True)
    mu = jnp.mean(v, axis=-1, keepdims=True)
    vc = v - mu
    vn = vc * lax.rsqrt(jnp.mean(vc * vc, axis=-1, keepdims=True) + EPS) * lng_ref[...] + lnb_ref[...]
    if emit_v:
        v_ref[...] = vn
    vb = vn.astype(BF16)
    for c in range(tm // lc):
        for g in range(N_SGU_GROUPS):
            cols = slice(g * SGU_GROUP_CH, (g + 1) * SGU_GROUP_CH)
            s_buf[c * lc:(c + 1) * lc, cols] = jnp.dot(
                wm_ref[g], vb[c * lc:(c + 1) * lc, cols], preferred_element_type=F32) + bias_ref[:, cols]
    u = jax.nn.gelu(proj(0), approximate=True)
    m = (u * s_buf[...] * _silu(proj(2))).astype(BF16)
    xn = x + gate_ref[...] * jnp.dot(m, wo_ref[...], preferred_element_type=F32)
    if final:
        xn = xn * lax.rsqrt(jnp.mean(xn * xn, axis=-1, keepdims=True) + EPS) * fg_ref[...]
    y_ref[...] = xn


def _odd_call(x, sc1p, sh, gate, g, win, wm, bias, lng, lnb, wo, fg, *, tm, final, emit_v):
    nb, t_len, d = x.shape
    lc = wm.shape[-1]
    kern = functools.partial(_odd_kernel, tm=tm, lc=lc, final=final, emit_v=emit_v)
    tok = pl.BlockSpec((None, tm, d), lambda b, t: (b, t, 0))
    per_b = pl.BlockSpec((None, 1, d), lambda b, t: (b, 0, 0))
    full = lambda a: pl.BlockSpec(a.shape, lambda b, t: (0,) * a.ndim)
    out_shape = [jax.ShapeDtypeStruct((nb, t_len, d), F32)]
    out_specs = [tok]
    if emit_v:
        out_shape.append(jax.ShapeDtypeStruct((nb, t_len, SGU_DIM), F32))
        out_specs.append(tok)
    res = pl.pallas_call(
        kern,
        grid=(nb, t_len // tm),
        in_specs=[tok, per_b, per_b, per_b, full(g), full(win), full(wm), full(bias), full(lng), full(lnb),
                  full(wo), full(fg)],
        out_specs=out_specs,
        out_shape=out_shape,
        scratch_shapes=[pltpu.VMEM((tm, SGU_DIM), F32)],
        compiler_params=pltpu.CompilerParams(
            dimension_semantics=("arbitrary", "arbitrary"), vmem_limit_bytes=VMEM_LIMIT_BYTES),
        name="odd_sgu",
    )(x, sc1p, sh, gate, g, win, wm, bias, lng, lnb, wo, fg)
    return (res[0], res[1]) if emit_v else (res[0], None)


def _pack_even_weights(w_in):
    ad, kv, idx = ATTN_DIM, N_KV * HEAD_DIM, N_IDX_HEADS * IDX_DIM
    cuts = [0]
    for s in (ad, kv, kv, ad, idx, IDX_DIM, N_IDX_HEADS, CONV_DIM, CONV_DIM, CONV_DIM, CONV_DIM):
        cuts.append(cuts[-1] + s)
    q, k, v, ga, qi, ki, wi, gb, gc, xin, gz = [w_in[:, cuts[i]:cuts[i + 1]] for i in range(11)]
    d = w_in.shape[0]
    wnn = jnp.concatenate([gb, gc, xin, gz, k, v, ki, jnp.zeros((d, NN_COLS - NN_KI - IDX_DIM), w_in.dtype)],
                          axis=1).astype(BF16)
    wnt = jnp.concatenate([q, ga, qi, v, wi, jnp.zeros((d, NT_ROWS - NT_WI - N_IDX_HEADS), w_in.dtype)],
                          axis=1).T.astype(BF16)
    return wnn, wnt


def _row(a):
    return a.reshape(1, -1)


def _cached_value_tiles(pv, ks):
    nb, p_len, _ = pv.shape
    t = pv.astype(BF16).reshape(nb, p_len // ks, ks, N_KV, HEAD_DIM)
    t = jnp.transpose(t, (0, 1, 3, 4, 2))
    ones = jnp.ones((nb, p_len // ks, N_KV, BF16_ROWS, ks), BF16)
    return jnp.concatenate([t, ones], axis=3).reshape(nb, p_len // ks, N_KV * V_ROWS, ks)


def _trunk(x, mod, norm_g, ev_w, ev_conv_w, ev_w_out, od_w_in, od_wm, od_bias, od_ln_g, od_ln_b, od_w_out,
           final_g, cache, *, tm_even, tm_odd, tq, ks):
    nb, t_len, d = x.shape
    depth = mod.shape[0]
    ks_s, vs_s, kis, convs, cvs = [], [], [], [], []
    for l in range(depth):
        shift = mod[l, :, None, 0:d]
        sc1p = 1.0 + mod[l, :, None, d:2 * d]
        gate = mod[l, :, None, 2 * d:3 * d]
        g = _row(norm_g[l])
        if l % 2 == 0:
            e = l // 2
            wnn, wnt = ev_w[e]
            if cache is None:
                hist = jnp.zeros((nb, CONV_W - 1, CONV_DIM), F32)
            else:
                hist = cache[3][e]
            vc = min(ks, tm_even)
            (qT, gaT, qiT, vT, wiT, k, v, ki, kb, kib, bo, cst) = _even_in_call(
                x, sc1p, sh=shift, g=g, wnn=wnn, wnt=wnt, cw=ev_conv_w[e], hist=hist, tm=tm_even, vc=vc)
            if cache is None:
                past, n_keys = 0, t_len
            else:
                pk, pv, pki = cache[0][e], cache[1][e], cache[2][e]
                past = pk.shape[1]
                n_keys = past + t_len
                n_pad = -n_keys % ks
                pkb = jnp.moveaxis(pk.astype(BF16), 2, 1)
                kb = jnp.pad(jnp.concatenate([pkb, kb], axis=2), ((0, 0), (0, 0), (0, n_pad), (0, 0)))
                kib = jnp.pad(jnp.concatenate([pki.astype(BF16), kib], axis=1), ((0, 0), (0, n_pad), (0, 0)))
                vT_new = jnp.pad(vT, ((0, 0), (0, 0), (0, 0), (0, ks - vc)))
                vT = jnp.concatenate([_cached_value_tiles(pv, ks), vT_new], axis=1)
            x = _attn_call(qT, gaT, qiT, wiT, kb, kib, vT, x, bo, gate, ev_w_out[e],
                           tq=tq, past=past, n_keys=n_keys)
            ks_s.append(k.reshape(nb, t_len, N_KV, HEAD_DIM))
            vs_s.append(v.reshape(nb, t_len, N_KV, HEAD_DIM))
            kis.append(ki)
            convs.append(cst)
        else:
            o = l // 2
            x, vrow = _odd_call(x, sc1p, shift, gate, g, od_w_in[o], od_wm[o], od_bias[o], _row(od_ln_g[o]),
                                _row(od_ln_b[o]), od_w_out[o], _row(final_g), tm=tm_odd,
                                final=(l == depth - 1), emit_v=cache is not None)
            cvs.append(vrow)
    new_c = None if cache is None else jnp.stack(cvs)
    return x, jnp.stack(ks_s), jnp.stack(vs_s), jnp.stack(kis), jnp.stack(convs), new_c


def _sgu_weights(od_ws, od_bs, lc):
    wm = jnp.tril(od_ws[:, :, :lc, :lc]).astype(BF16)
    bias = jnp.repeat(jnp.swapaxes(od_bs[:, :, :lc], 1, 2), SGU_GROUP_CH, axis=2)
    return wm, bias


def kernel(x_prompt, x_sample, cache_a_k, cache_a_v, cache_a_kidx, state_b_conv, c_prompt, c_sample,
           ada_w, ada_b, norm_g, ev_w_in, ev_conv_w, ev_w_out, od_w_in, od_ws, od_bs, od_ln_g, od_ln_b,
           od_w_out, final_g):
    nbp, tp, _ = x_prompt.shape
    nbs, ts, _ = x_sample.shape
    n_even = ev_w_in.shape[0]

    mod = _ada_call(jnp.concatenate([c_prompt, c_sample], axis=0), ada_w, ada_b)
    ev_w = [_pack_even_weights(ev_w_in[e]) for e in range(n_even)]
    ev_wo = ev_w_out.astype(BF16)
    od_wi = od_w_in.astype(BF16)
    od_wo = od_w_out.astype(BF16)
    past_len = cache_a_k.shape[2]
    cache = (cache_a_k, cache_a_v.reshape(n_even, nbs, past_len, N_KV * HEAD_DIM), cache_a_kidx, state_b_conv)

    def run(x, mod_part, cache, tm_even, tm_odd, tq, ks):
        t_len = x.shape[1]
        lc = min(SGU_CHUNK, t_len)
        wm, bias = _sgu_weights(od_ws, od_bs, lc)
        return _trunk(x, mod_part, norm_g, ev_w, ev_conv_w, ev_wo, od_wi, wm, bias, od_ln_g, od_ln_b, od_wo,
                      final_g, cache, tm_even=tm_even, tm_odd=tm_odd, tq=tq, ks=ks)

    y_p, k_p, v_p, ki_p, conv_p, _ = run(x_prompt, mod[:, :nbp], None, 512, 512, 256, 256)
    y_s, k_s, v_s, ki_s, conv_s, cv_s = run(x_sample, mod[:, nbp:], cache, ts, ts, ts, 128)
    return (y_p, y_s, k_p, v_p, ki_p, conv_p, k_s, v_s, ki_s, conv_s, cv_s)
```

```python
import functools
import math

import jax
import jax.numpy as jnp
from jax import lax
from jax.experimental import pallas as pl
from jax.experimental.pallas import tpu as pltpu

F32 = jnp.float32
BF16 = jnp.bfloat16

D_MODEL = 1024
EPS = 1e-6
CHUNK_SHIFT = 6
ATTN_DIM = 512
N_HEADS = 8
HEAD_DIM = 64
N_KV = 2
GQA_REP = N_HEADS // N_KV
N_IDX_HEADS = 8
IDX_DIM = 64
TOPK_MAX = 256
CONV_DIM = 512
CONV_W = 3
SGU_DIM = 1024
SGU_CHUNK = 128
N_SGU_GROUPS = 8
SGU_GROUP_CH = SGU_DIM // N_SGU_GROUPS

SUBLANES = 8
LANES = 128
BF16_ROWS = 16
VMEM_LIMIT_BYTES = 52 * 1024 * 1024

BISECT_STEPS = 14
PV_CHUNK_TILES = 4
NEG_SCORE = -3.0e38
POS_SCORE = 3.0e38
NEG_BIAS = -1.0e30
LOG2E = 1.4426950408889634

V_ROWS = HEAD_DIM + BF16_ROWS

NT_Q, NT_GA, NT_QI, NT_V, NT_WI, NT_ROWS = 0, 512, 1024, 1536, 1664, 1680
NN_GB, NN_GC, NN_XIN, NN_GZ, NN_K, NN_V, NN_KI, NN_COLS = 0, 512, 1024, 1536, 2048, 2176, 2304, 2432


def _silu(x):
    return x / (1.0 + jnp.exp(-x))


def _fold_rows(a, op):
    r, c = a.shape
    n = r // SUBLANES
    if n % 4 == 0 and n >= 8:
        return op(op(a.reshape(4, n // 4, SUBLANES, c), axis=1), axis=0)
    return op(a.reshape(n, SUBLANES, c), axis=0)


def _modulated_norm(x, g, sc1p, sh):
    ms = jnp.mean(x * x, axis=-1, keepdims=True)
    return (x * lax.rsqrt(ms + EPS) * g) * sc1p + sh


def _ada_kernel(c_ref, w_ref, b_ref, o_ref):
    cs = _silu(c_ref[...])
    o_ref[...] = jnp.dot(cs, w_ref[...], preferred_element_type=F32,
                         precision=lax.Precision.HIGHEST) + b_ref[...]


def _ada_call(c_all, ada_w, ada_b):
    depth, d, n3 = ada_w.shape
    nb = c_all.shape[0]
    bn = 1024
    return pl.pallas_call(
        _ada_kernel,
        grid=(depth, n3 // bn),
        in_specs=[
            pl.BlockSpec((nb, d), lambda l, n: (0, 0)),
            pl.BlockSpec((None, d, bn), lambda l, n: (l, 0, n)),
            pl.BlockSpec((None, 1, bn), lambda l, n: (l, 0, n)),
        ],
        out_specs=pl.BlockSpec((None, nb, bn), lambda l, n: (l, 0, n)),
        out_shape=jax.ShapeDtypeStruct((depth, nb, n3), F32),
        compiler_params=pltpu.CompilerParams(
            dimension_semantics=("arbitrary", "arbitrary"), vmem_limit_bytes=VMEM_LIMIT_BYTES),
        name="ada_mod",
    )(c_all, ada_w, ada_b.reshape(depth, 1, n3))


def _even_in_kernel(x_ref, sc_ref, sh_ref, g_ref, wnn_ref, wnt_ref, cw_ref, hist_ref,
                    qT_ref, gaT_ref, qiT_ref, vT_ref, wiT_ref, k_ref, v_ref, ki_ref,
                    kb_ref, kib_ref, bo_ref, cst_ref, zbuf, *, tm, vc):
    t = pl.program_id(1)
    hb = _modulated_norm(x_ref[...], g_ref[...], sc_ref[...], sh_ref[...]).astype(BF16)

    def nn(c0, c1):
        return jnp.dot(hb, wnn_ref[:, c0:c1], preferred_element_type=F32)

    def nt(r0, r1):
        return lax.dot_general(wnt_ref[r0:r1, :], hb, (((1,), (1,)), ((), ())),
                               preferred_element_type=F32)

    z = nn(NN_GC, NN_XIN) * nn(NN_XIN, NN_GZ)

    @pl.when(t == 0)
    def _():
        zbuf[0:SUBLANES, :] = jnp.zeros((SUBLANES, CONV_DIM), F32)
        zbuf[SUBLANES - 2:SUBLANES, :] = hist_ref[...]

    zbuf[SUBLANES:SUBLANES + tm, :] = z
    z1 = zbuf[SUBLANES - 1:SUBLANES - 1 + tm, :]
    z2 = zbuf[SUBLANES - 2:SUBLANES - 2 + tm, :]
    y = cw_ref[2:3, :] * z + cw_ref[1:2, :] * z1 + cw_ref[0:1, :] * z2
    last2 = zbuf[SUBLANES + tm - 2:SUBLANES + tm, :]
    zbuf[SUBLANES - 2:SUBLANES, :] = last2
    cst_ref[...] = last2
    bo_ref[...] = (nn(NN_GB, NN_GC) * y * _silu(nn(NN_GZ, NN_K))).astype(BF16)

    kvk = nn(NN_K, NN_COLS)
    k = kvk[:, 0:128]
    k_ref[...] = k
    v_ref[...] = kvk[:, 128:256]
    ki = kvk[:, 256:256 + IDX_DIM]
    ki_ref[...] = ki
    kib_ref[...] = ki.astype(BF16)
    for g in range(N_KV):
        kb_ref[g] = k[:, g * HEAD_DIM:(g + 1) * HEAD_DIM].astype(BF16)

    qT_ref[...] = (nt(NT_Q, NT_GA) * (HEAD_DIM ** -0.5 * LOG2E)).astype(BF16)
    gaT_ref[...] = _silu(nt(NT_GA, NT_QI))
    qiT_ref[...] = nt(NT_QI, NT_V).astype(BF16)
    vT = nt(NT_V, NT_WI).astype(BF16)
    ones = jnp.ones((BF16_ROWS, vc), BF16)
    for c in range(tm // vc):
        for g in range(N_KV):
            vT_ref[c, g * V_ROWS:g * V_ROWS + HEAD_DIM, :] = vT[g * HEAD_DIM:(g + 1) * HEAD_DIM, c * vc:(c + 1) * vc]
            vT_ref[c, g * V_ROWS + HEAD_DIM:(g + 1) * V_ROWS, :] = ones
    wiT_ref[...] = nt(NT_WI, NT_ROWS)[0:N_IDX_HEADS, :]


def _even_in_call(x, sc1p, sh, g, wnn, wnt, cw, hist, *, tm, vc):
    nb, t_len, d = x.shape
    nt_tiles = t_len // tm
    kern = functools.partial(_even_in_kernel, tm=tm, vc=vc)
    tok = lambda w: pl.BlockSpec((None, tm, w), lambda b, t: (b, t, 0))
    tokT = lambda r: pl.BlockSpec((None, r, tm), lambda b, t: (b, 0, t))
    per_b = lambda r, w: pl.BlockSpec((None, r, w), lambda b, t: (b, 0, 0))
    full2 = lambda a: pl.BlockSpec(a.shape, lambda b, t: (0, 0))
    out_shape = (
        jax.ShapeDtypeStruct((nb, ATTN_DIM, t_len), BF16),
        jax.ShapeDtypeStruct((nb, ATTN_DIM, t_len), F32),
        jax.ShapeDtypeStruct((nb, N_IDX_HEADS * IDX_DIM, t_len), BF16),
        jax.ShapeDtypeStruct((nb, t_len // vc, N_KV * V_ROWS, vc), BF16),
        jax.ShapeDtypeStruct((nb, N_IDX_HEADS, t_len), F32),
        jax.ShapeDtypeStruct((nb, t_len, N_KV * HEAD_DIM), F32),
        jax.ShapeDtypeStruct((nb, t_len, N_KV * HEAD_DIM), F32),
        jax.ShapeDtypeStruct((nb, t_len, IDX_DIM), F32),
        jax.ShapeDtypeStruct((nb, N_KV, t_len, HEAD_DIM), BF16),
        jax.ShapeDtypeStruct((nb, t_len, IDX_DIM), BF16),
        jax.ShapeDtypeStruct((nb, t_len, CONV_DIM), BF16),
        jax.ShapeDtypeStruct((nb, CONV_W - 1, CONV_DIM), F32),
    )
    out_specs = (
        tokT(ATTN_DIM), tokT(ATTN_DIM), tokT(N_IDX_HEADS * IDX_DIM),
        pl.BlockSpec((None, tm // vc, N_KV * V_ROWS, vc), lambda b, t: (b, t, 0, 0)),
        tokT(N_IDX_HEADS),
        tok(N_KV * HEAD_DIM), tok(N_KV * HEAD_DIM), tok(IDX_DIM),
        pl.BlockSpec((None, N_KV, tm, HEAD_DIM), lambda b, t: (b, 0, t, 0)),
        tok(IDX_DIM), tok(CONV_DIM), per_b(CONV_W - 1, CONV_DIM),
    )
    return pl.pallas_call(
        kern,
        grid=(nb, nt_tiles),
        in_specs=[tok(d), per_b(1, d), per_b(1, d), full2(g), full2(wnn), full2(wnt), full2(cw),
                  per_b(CONV_W - 1, CONV_DIM)],
        out_specs=out_specs,
        out_shape=out_shape,
        scratch_shapes=[pltpu.VMEM((tm + SUBLANES, CONV_DIM), F32)],
        compiler_params=pltpu.CompilerParams(
            dimension_semantics=("arbitrary", "arbitrary"), vmem_limit_bytes=VMEM_LIMIT_BYTES),
        name="even_in",
    )(x, sc1p, sh, g, wnn, wnt, cw, hist)


def _attn_block(jb, refs, *, tq, ks, past, n_keys, topk):
    (qT_ref, gaT_ref, qiT_ref, wiT_ref, kb_ref, kib_ref, vT_ref, x_ref, bo_ref, gate_ref, wo_ref,
     y_ref, s_ref, lg_ref, acc_ref) = refs
    q0 = past + jb * tq
    limit = min((((q0 + tq - 1) >> CHUNK_SHIFT) + 1) << CHUNK_SHIFT, n_keys)
    n_st = -(-limit // ks)
    n_full = ((q0 >> CHUNK_SHIFT) << CHUNK_SHIFT) // ks
    n_rows = n_st * ks
    kf = float(topk)
    qpos = q0 + lax.broadcasted_iota(jnp.int32, (1, tq), 1)
    qchunk = qpos >> CHUNK_SHIFT
    row_iota = lax.broadcasted_iota(jnp.int32, (ks, tq), 0)

    def tile_off(st):
        return pl.multiple_of(st * ks, ks)

    def score_tile(off):
        kib = kib_ref[pl.ds(off, ks), :]
        s = None
        for h in range(N_IDX_HEADS):
            rel = jnp.dot(kib, qiT_ref[h * IDX_DIM:(h + 1) * IDX_DIM, :], preferred_element_type=F32)
            term = wiT_ref[h:h + 1, :] * jnp.maximum(rel, 0.0)
            s = term if s is None else s + term
        return s

    def full_body(st, carry):
        mn, mx = carry
        off = tile_off(st)
        s = score_tile(off)
        s_ref[pl.ds(off, ks), :] = s
        return jnp.minimum(mn, _fold_rows(s, jnp.min)), jnp.maximum(mx, _fold_rows(s, jnp.max))

    mn, mx = lax.fori_loop(
        0, n_full, full_body,
        (jnp.full((SUBLANES, tq), POS_SCORE, F32), jnp.full((SUBLANES, tq), NEG_SCORE, F32)), unroll=2)
    for st in range(n_full, n_st):
        off = st * ks
        s = score_tile(off)
        spos = row_iota + off
        adm = ((spos >> CHUNK_SHIFT) <= qchunk) & (spos < n_keys)
        s_lo = jnp.where(adm, s, NEG_SCORE)
        s_ref[off:off + ks, :] = s_lo
        mn = jnp.minimum(mn, _fold_rows(jnp.where(adm, s, POS_SCORE), jnp.min))
        mx = jnp.maximum(mx, _fold_rows(s_lo, jnp.max))
    lo0 = jnp.min(mn, axis=0, keepdims=True)
    mx0 = jnp.max(mx, axis=0, keepdims=True)
    hi0 = mx0 + jnp.maximum(jnp.abs(mx0), 1e-30) * (2.0 ** -10)
    n_adm = jnp.minimum((qchunk + 1) << CHUNK_SHIFT, n_keys).astype(F32)

    def tiles():
        for st in range(n_st):
            yield st * ks, s_ref[st * ks:(st + 1) * ks, :]

    def count_where(pred):
        tot = None
        for off, s in tiles():
            c = _fold_rows(jnp.where(pred(off, s), 1.0, 0.0), jnp.sum)
            tot = c if tot is None else tot + c
        return jnp.sum(tot, axis=0, keepdims=True)

    def bisect_body(_, carry):
        lo, hi, c_lo, c_hi = carry
        mid = 0.5 * (lo + hi)
        c = count_where(lambda off, s: s >= mid)
        ge = c >= kf
        return (jnp.where(ge, mid, lo), jnp.where(ge, hi, mid),
                jnp.where(ge, c, c_lo), jnp.where(ge, c_hi, c))

    lo, hi, c_lo, c_hi = lax.fori_loop(
        0, BISECT_STEPS, bisect_body, (lo0, hi0, n_adm, jnp.zeros((1, tq), F32)))

    def peel_body(carry):
        lo, hi, c_lo, c_hi, done, _ = carry
        top = None
        for off, s in tiles():
            m = _fold_rows(jnp.where(s < hi, s, NEG_SCORE), jnp.max)
            top = m if top is None else jnp.maximum(top, m)
        top = jnp.max(top, axis=0, keepdims=True)
        c = count_where(lambda off, s: s >= top)
        live = done < 0.5
        fin = live & (c >= kf)
        down = live & (c < kf)
        lo = jnp.where(fin, top, lo)
        c_lo = jnp.where(fin, c, c_lo)
        hi = jnp.where(down, top, hi)
        c_hi = jnp.where(down, c, c_hi)
        done = jnp.where(fin, 1.0, done)
        return lo, hi, c_lo, c_hi, done, jnp.sum(1.0 - done)

    done0 = jnp.where(c_lo <= kf, 1.0, 0.0)
    lo, hi, c_lo, c_hi, _, _ = lax.while_loop(
        lambda carry: carry[5] > 0.0, peel_body, (lo, hi, c_lo, c_hi, done0, jnp.sum(1.0 - done0)))

    need = kf - c_hi
    n_j_steps = max(1, n_rows.bit_length())

    def tie_cut():
        def jbody(_, carry):
            jlo, jhi = carry
            jmid = (jlo + jhi) >> 1
            c = count_where(lambda off, s: (s >= lo) & (s < hi) & ((row_iota + off) <= jmid))
            ok = c >= need
            return jnp.where(ok, jlo, jmid), jnp.where(ok, jmid, jhi)

        _, jhi = lax.fori_loop(
            0, n_j_steps, jbody,
            (jnp.full((1, tq), -1, jnp.int32), jnp.full((1, tq), n_rows - 1, jnp.int32)))
        return jhi

    has_tie = jnp.max(c_lo) > kf
    jcut = lax.cond(has_tie, tie_cut, lambda: jnp.full((1, tq), n_rows - 1, jnp.int32))

    for off, s in tiles():
        sel = (s >= lo) & ((s >= hi) | ((row_iota + off) <= jcut))
        s_ref[off:off + ks, :] = jnp.where(sel, 0.0, NEG_BIAS)

    qTg = [jnp.concatenate(
        [qT_ref[(g * GQA_REP + r) * HEAD_DIM:(g * GQA_REP + r + 1) * HEAD_DIM, :] for r in range(GQA_REP)],
        axis=1) for g in range(N_KV)]

    def logit_body(st, m8s):
        off = tile_off(st)
        b = s_ref[pl.ds(off, ks), :]
        b4 = jnp.concatenate([b] * GQA_REP, axis=1)
        out = []
        for g in range(N_KV):
            lt = jnp.dot(kb_ref[g, pl.ds(off, ks), :], qTg[g], preferred_element_type=F32) + b4
            lg_ref[g, pl.ds(off, ks), :] = lt
            out.append(jnp.maximum(m8s[g], _fold_rows(lt, jnp.max)))
        return tuple(out)

    m8s = lax.fori_loop(0, n_st, logit_body,
                        tuple(jnp.full((SUBLANES, GQA_REP * tq), NEG_SCORE, F32) for _ in range(N_KV)),
                        unroll=2)
    ms = [jnp.max(m8, axis=0, keepdims=True) for m8 in m8s]
    accs = [None] * N_KV
    st0 = 0
    while st0 < n_st:
        n_ch = min(PV_CHUNK_TILES, n_st - st0)
        rows = slice(st0 * ks, (st0 + n_ch) * ks)
        for g in range(N_KV):
            p = jnp.exp2((lg_ref[g, rows, :] - ms[g]).astype(BF16))
            vcat = jnp.concatenate([vT_ref[st0 + c, g * V_ROWS:(g + 1) * V_ROWS, :] for c in range(n_ch)], axis=1)
            d = jnp.dot(vcat, p, preferred_element_type=F32)
            accs[g] = d if accs[g] is None else accs[g] + d
        st0 += n_ch
    o_rows = []
    for g in range(N_KV):
        o = accs[g][0:HEAD_DIM, :] / accs[g][HEAD_DIM:HEAD_DIM + 1, :]
        for r in range(GQA_REP):
            o_rows.append(o[:, r * tq:(r + 1) * tq])
    oT = jnp.concatenate(o_rows, axis=0)

    a = (oT * gaT_ref[...]).T.astype(BF16)
    out = jnp.dot(a, wo_ref[0:ATTN_DIM, :], preferred_element_type=F32)
    out = out + jnp.dot(bo_ref[...], wo_ref[ATTN_DIM:ATTN_DIM + CONV_DIM, :], preferred_element_type=F32)
    y_ref[...] = x_ref[...] + gate_ref[...] * out


def _attn_kernel(*refs, nq, **static):
    if nq == 1:
        _attn_block(0, refs, **static)
        return
    j = pl.program_id(1)
    for jb in range(nq):
        pl.when(j == jb)(functools.partial(_attn_block, jb, refs, **static))


def _attn_call(qT, gaT, qiT, wiT, kb, kib, vT, x, bo, gate, wo, *, tq, past, n_keys):
    nb, t_len, d = x.shape
    n_keys_pad = kb.shape[2]
    ks = vT.shape[3]
    nq = t_len // tq
    topk = min(TOPK_MAX, n_keys // 4)
    kern = functools.partial(_attn_kernel, nq=nq, tq=tq, ks=ks, past=past, n_keys=n_keys, topk=topk)
    tokT = lambda r: pl.BlockSpec((None, r, tq), lambda b, j: (b, 0, j))
    tok = lambda w: pl.BlockSpec((None, tq, w), lambda b, j: (b, j, 0))
    return pl.pallas_call(
        kern,
        grid=(nb, nq),
        in_specs=[
            tokT(ATTN_DIM), tokT(ATTN_DIM), tokT(N_IDX_HEADS * IDX_DIM), tokT(N_IDX_HEADS),
            pl.BlockSpec((None, N_KV, n_keys_pad, HEAD_DIM), lambda b, j: (b, 0, 0, 0)),
            pl.BlockSpec((None, n_keys_pad, IDX_DIM), lambda b, j: (b, 0, 0)),
            pl.BlockSpec((None, n_keys_pad // ks, N_KV * V_ROWS, ks), lambda b, j: (b, 0, 0, 0)),
            tok(d), tok(CONV_DIM),
            pl.BlockSpec((None, 1, d), lambda b, j: (b, 0, 0)),
            pl.BlockSpec(wo.shape, lambda b, j: (0, 0)),
        ],
        out_specs=tok(d),
        out_shape=jax.ShapeDtypeStruct((nb, t_len, d), F32),
        scratch_shapes=[pltpu.VMEM((n_keys_pad, tq), F32),
                        pltpu.VMEM((N_KV, n_keys_pad, GQA_REP * tq), F32),
                        pltpu.VMEM((N_KV, V_ROWS, GQA_REP * tq), F32)],
        compiler_params=pltpu.CompilerParams(
            dimension_semantics=("arbitrary", "arbitrary"), vmem_limit_bytes=VMEM_LIMIT_BYTES),
        name="dsa_attn",
    )(qT, gaT, qiT, wiT, kb, kib, vT, x, bo, gate, wo)


def _odd_kernel(x_ref, sc_ref, sh_ref, gate_ref, g_ref, win_ref, wm_ref, bias_ref, lng_ref, lnb_ref,
                wo_ref, fg_ref, y_ref, *rest, tm, lc, final, emit_v):
    if emit_v:
        v_ref, s_buf = rest
    else:
        (s_buf,) = rest
    x = x_ref[...]
    hb = _modulated_norm(x, g_ref[...], sc_ref[...], sh_ref[...]).astype(BF16)

    def proj(i):
        return jnp.dot(hb, win_ref[:, i * SGU_DIM:(i + 1) * SGU_DIM], preferred_element_type=F32)

    v = jax.nn.gelu(proj(1), approximate=True)
    mu = jnp.mean(v, axis=-1, keepdims=True)
    vc = v - mu
    vn = vc * lax.rsqrt(jnp.mean(vc * vc, axis=-1, keepdims=True) + EPS) * lng_ref[...] + lnb_ref[...]
    if emit_v:
        v_ref[...] = vn
    vb = vn.astype(BF16)
    for c in range(tm // lc):
        for g in range(N_SGU_GROUPS):
            cols = slice(g * SGU_GROUP_CH, (g + 1) * SGU_GROUP_CH)
            s_buf[c * lc:(c + 1) * lc, cols] = jnp.dot(
                wm_ref[g], vb[c * lc:(c + 1) * lc, cols], preferred_element_type=F32) + bias_ref[:, cols]
    u = jax.nn.gelu(proj(0), approximate=True)
    m = (u * s_buf[...] * _silu(proj(2))).astype(BF16)
    xn = x + gate_ref[...] * jnp.dot(m, wo_ref[...], preferred_element_type=F32)
    if final:
        xn = xn * lax.rsqrt(jnp.mean(xn * xn, axis=-1, keepdims=True) + EPS) * fg_ref[...]
    y_ref[...] = xn


def _odd_call(x, sc1p, sh, gate, g, win, wm, bias, lng, lnb, wo, fg, *, tm, final, emit_v):
    nb, t_len, d = x.shape
    lc = wm.shape[-1]
    kern = functools.partial(_odd_kernel, tm=tm, lc=lc, final=final, emit_v=emit_v)
    tok = pl.BlockSpec((None, tm, d), lambda b, t: (b, t, 0))
    per_b = pl.BlockSpec((None, 1, d), lambda b, t: (b, 0, 0))
    full = lambda a: pl.BlockSpec(a.shape, lambda b, t: (0,) * a.ndim)
    out_shape = [jax.ShapeDtypeStruct((nb, t_len, d), F32)]
    out_specs = [tok]
    if emit_v:
        out_shape.append(jax.ShapeDtypeStruct((nb, t_len, SGU_DIM), F32))
        out_specs.append(tok)
    res = pl.pallas_call(
        kern,
        grid=(nb, t_len // tm),
        in_specs=[tok, per_b, per_b, per_b, full(g), full(win), full(wm), full(bias), full(lng), full(lnb),
                  full(wo), full(fg)],
        out_specs=out_specs,
        out_shape=out_shape,
        scratch_shapes=[pltpu.VMEM((tm, SGU_DIM), F32)],
        compiler_params=pltpu.CompilerParams(
            dimension_semantics=("arbitrary", "arbitrary"), vmem_limit_bytes=VMEM_LIMIT_BYTES),
        name="odd_sgu",
    )(x, sc1p, sh, gate, g, win, wm, bias, lng, lnb, wo, fg)
    return (res[0], res[1]) if emit_v else (res[0], None)


def _pack_even_weights(w_in):
    ad, kv, idx = ATTN_DIM, N_KV * HEAD_DIM, N_IDX_HEADS * IDX_DIM
    cuts = [0]
    for s in (ad, kv, kv, ad, idx, IDX_DIM, N_IDX_HEADS, CONV_DIM, CONV_DIM, CONV_DIM, CONV_DIM):
        cuts.append(cuts[-1] + s)
    q, k, v, ga, qi, ki, wi, gb, gc, xin, gz = [w_in[:, cuts[i]:cuts[i + 1]] for i in range(11)]
    d = w_in.shape[0]
    wnn = jnp.concatenate([gb, gc, xin, gz, k, v, ki, jnp.zeros((d, NN_COLS - NN_KI - IDX_DIM), w_in.dtype)],
                          axis=1).astype(BF16)
    wnt = jnp.concatenate([q, ga, qi, v, wi, jnp.zeros((d, NT_ROWS - NT_WI - N_IDX_HEADS), w_in.dtype)],
                          axis=1).T.astype(BF16)
    return wnn, wnt


def _row(a):
    return a.reshape(1, -1)


def _cached_value_tiles(pv, ks):
    nb, p_len, _ = pv.shape
    t = pv.astype(BF16).reshape(nb, p_len // ks, ks, N_KV, HEAD_DIM)
    t = jnp.transpose(t, (0, 1, 3, 4, 2))
    ones = jnp.ones((nb, p_len // ks, N_KV, BF16_ROWS, ks), BF16)
    return jnp.concatenate([t, ones], axis=3).reshape(nb, p_len // ks, N_KV * V_ROWS, ks)


def _trunk(x, mod, norm_g, ev_w, ev_conv_w, ev_w_out, od_w_in, od_wm, od_bias, od_ln_g, od_ln_b, od_w_out,
           final_g, cache, *, tm_even, tm_odd, tq, ks):
    nb, t_len, d = x.shape
    depth = mod.shape[0]
    ks_s, vs_s, kis, convs, cvs = [], [], [], [], []
    for l in range(depth):
        shift = mod[l, :, None, 0:d]
        sc1p = 1.0 + mod[l, :, None, d:2 * d]
        gate = mod[l, :, None, 2 * d:3 * d]
        g = _row(norm_g[l])
        if l % 2 == 0:
            e = l // 2
            wnn, wnt = ev_w[e]
            if cache is None:
                hist = jnp.zeros((nb, CONV_W - 1, CONV_DIM), F32)
            else:
                hist = cache[3][e]
            vc = min(ks, tm_even)
            (qT, gaT, qiT, vT, wiT, k, v, ki, kb, kib, bo, cst) = _even_in_call(
                x, sc1p, sh=shift, g=g, wnn=wnn, wnt=wnt, cw=ev_conv_w[e], hist=hist, tm=tm_even, vc=vc)
            if cache is None:
                past, n_keys = 0, t_len
            else:
                pk, pv, pki = cache[0][e], cache[1][e], cache[2][e]
                past = pk.shape[1]
                n_keys = past + t_len
                n_pad = -n_keys % ks
                pkb = jnp.moveaxis(pk.astype(BF16), 2, 1)
                kb = jnp.pad(jnp.concatenate([pkb, kb], axis=2), ((0, 0), (0, 0), (0, n_pad), (0, 0)))
                kib = jnp.pad(jnp.concatenate([pki.astype(BF16), kib], axis=1), ((0, 0), (0, n_pad), (0, 0)))
                vT_new = jnp.pad(vT, ((0, 0), (0, 0), (0, 0), (0, ks - vc)))
                vT = jnp.concatenate([_cached_value_tiles(pv, ks), vT_new], axis=1)
            x = _attn_call(qT, gaT, qiT, wiT, kb, kib, vT, x, bo, gate, ev_w_out[e],
                           tq=tq, past=past, n_keys=n_keys)
            ks_s.append(k.reshape(nb, t_len, N_KV, HEAD_DIM))
            vs_s.append(v.reshape(nb, t_len, N_KV, HEAD_DIM))
            kis.append(ki)
            convs.append(cst)
        else:
            o = l // 2
            x, vrow = _odd_call(x, sc1p, shift, gate, g, od_w_in[o], od_wm[o], od_bias[o], _row(od_ln_g[o]),
                                _row(od_ln_b[o]), od_w_out[o], _row(final_g), tm=tm_odd,
                                final=(l == depth - 1), emit_v=cache is not None)
            cvs.append(vrow)
    new_c = None if cache is None else jnp.stack(cvs)
    return x, jnp.stack(ks_s), jnp.stack(vs_s), jnp.stack(kis), jnp.stack(convs), new_c


def _sgu_weights(od_ws, od_bs, lc):
    wm = jnp.tril(od_ws[:, :, :lc, :lc]).astype(BF16)
    bias = jnp.repeat(jnp.swapaxes(od_bs[:, :, :lc], 1, 2), SGU_GROUP_CH, axis=2)
    return wm, bias


def kernel(x_prompt, x_sample, cache_a_k, cache_a_v, cache_a_kidx, state_b_conv, c_prompt, c_sample,
           ada_w, ada_b, norm_g, ev_w_in, ev_conv_w, ev_w_out, od_w_in, od_ws, od_bs, od_ln_g, od_ln_b,
           od_w_out, final_g):
    nbp, tp, _ = x_prompt.shape
    nbs, ts, _ = x_sample.shape
    n_even = ev_w_in.shape[0]

    mod = _ada_call(jnp.concatenate([c_prompt, c_sample], axis=0), ada_w, ada_b)
    ev_w = [_pack_even_weights(ev_w_in[e]) for e in range(n_even)]
    ev_wo = ev_w_out.astype(BF16)
    od_wi = od_w_in.astype(BF16)
    od_wo = od_w_out.astype(BF16)
    past_len = cache_a_k.shape[2]
    cache = (cache_a_k, cache_a_v.reshape(n_even, nbs, past_len, N_KV * HEAD_DIM), cache_a_kidx, state_b_conv)

    def run(x, mod_part, cache, tm_even, tm_odd, tq, ks):
        t_len = x.shape[1]
        lc = min(SGU_CHUNK, t_len)
        wm, bias = _sgu_weights(od_ws, od_bs, lc)
        return _trunk(x, mod_part, norm_g, ev_w, ev_conv_w, ev_wo, od_wi, wm, bias, od_ln_g, od_ln_b, od_wo,
                      final_g, cache, tm_even=tm_even, tm_odd=tm_odd, tq=tq, ks=ks)

    y_p, k_p, v_p, ki_p, conv_p, _ = run(x_prompt, mod[:, :nbp], None, 512, 512, 256, 256)
    y_s, k_s, v_s, ki_s, conv_s, cv_s = run(x_sample, mod[:, nbp:], cache, ts, ts, ts, 128)
    return (y_p, y_s, k_p, v_p, ki_p, conv_p, k_s, v_s, ki_s, conv_s, cv_s)
```
